```python
import math
import jax, jax.numpy as jnp
from jax import lax
import numpy as np

D_MODEL = 1024
BATCH = 4
SEQ = 4096
DEPTH = 2
DEC_BATCH = 32
DEC_SEQ = 4
PAST_LEN = 16384
PAGE_SIZE = 128

HEAD_DIM = 64
D_ATT = D_MODEL
N_ATT_HEADS = D_ATT // HEAD_DIM
D_SSM = D_MODEL
N_SSM_HEADS = D_SSM // HEAD_DIM
N_SSM_GROUPS = 2
D_STATE = 128
CONV_WIDTH = 4
CONV_DIM = D_SSM + 2 * N_SSM_GROUPS * D_STATE
SSD_CHUNK = 128
D_MIX = D_ATT + D_SSM
DILATED_CONFIGS = ((128, 1), (512, 4), (2048, 16))
MAX_WINDOW = 2048
ATT_BLOCK = 128
PEER_HEADS = 8
PEER_NKEYS = 128
PEER_EXPERTS = PEER_NKEYS * PEER_NKEYS
PEER_TOPK = 16
PEER_DKEY = 256
PEER_BLOCK = 128
LN_EPS = 1e-5
RMS_EPS = 1e-5
ALPHA = (2 * DEPTH) ** 0.25
BETA = (8 * DEPTH) ** -0.25
Q_OFF = 0
K_OFF = D_ATT
V_OFF = 2 * D_ATT
Z_OFF = 3 * D_ATT
XBC_OFF = Z_OFF + D_SSM
DT_OFF = XBC_OFF + CONV_DIM
N_IN = DT_OFF + N_SSM_HEADS

kernel_name = 'hybrid_ssd_dilated_peer_step'


def layer_norm(x, g, b):
    xf = x.astype(jnp.float32)
    mu = xf.mean(-1, keepdims=True)
    var = jnp.square(xf - mu).mean(-1, keepdims=True)
    return ((xf - mu) * lax.rsqrt(var + LN_EPS) * g.astype(jnp.float32) + b.astype(jnp.float32)).astype(x.dtype)


def alibi_slopes():
    return 2.0 ** (-8.0 * jnp.arange(1, N_ATT_HEADS + 1, dtype=jnp.float32) / N_ATT_HEADS)


def dilated_band_attention(q, k, v, window, dil, slopes):
    b, s, h, e = q.shape
    span = window // dil
    unit = dil * ATT_BLOCK
    s_pad = -(-s // unit) * unit
    nblk = s_pad // unit

    def to_blocks(a):
        a = jnp.pad(a, ((0, 0), (0, s_pad - s), (0, 0), (0, 0)))
        return a.reshape(b, nblk, ATT_BLOCK, dil, h, e)

    def with_prev(a):
        prev = jnp.concatenate([jnp.zeros_like(a[:, :1]), a[:, :-1]], axis=1)
        return jnp.concatenate([prev, a], axis=2)

    qb, kb, vb = to_blocks(q), to_blocks(k), to_blocks(v)
    kc, vc = with_prev(kb), with_prev(vb)
    sc = jnp.einsum('bnirhe,bnjrhe->bnrhij', qb, kc).astype(jnp.float32) * (e ** -0.5)
    qi = jnp.arange(ATT_BLOCK)[:, None]
    kj = jnp.arange(2 * ATT_BLOCK)[None, :]
    back = ATT_BLOCK + qi - kj
    key_m = jnp.arange(nblk)[:, None, None] * ATT_BLOCK + kj[None] - ATT_BLOCK
    valid = (back >= 0) & (back <= span) & (key_m >= 0)
    bias = -slopes[:, None, None] * (back * dil).astype(jnp.float32)
    sc = jnp.where(valid[:, None, None], sc + bias, -jnp.inf)
    lse = jax.nn.logsumexp(sc, axis=-1)
    p = jnp.exp(sc - lse[..., None]).astype(v.dtype)
    o = jnp.einsum('bnrhij,bnjrhe->bnirhe', p, vc).reshape(b, s_pad, h, e)[:, :s]
    lse = lse.transpose(0, 1, 4, 2, 3).reshape(b, s_pad, h)[:, :s]
    return o, lse


def dilated_gather_attention(q, k_all, v_all, window, dil, slopes):
    b, t, h, e = q.shape
    past = k_all.shape[1] - t
    span = window // dil
    steps = jnp.arange(span + 1)
    idx = past + jnp.arange(t)[:, None] - steps[None, :] * dil
    valid = idx >= 0
    idx = jnp.maximum(idx, 0)
    kg = k_all[:, idx]
    vg = v_all[:, idx]
    sc = jnp.einsum('bthe,btjhe->bhtj', q, kg).astype(jnp.float32) * (e ** -0.5)
    sc = sc - slopes[:, None, None] * (steps * dil).astype(jnp.float32)
    sc = jnp.where(valid, sc, -jnp.inf)
    lse = jax.nn.logsumexp(sc, axis=-1)
    p = jnp.exp(sc - lse[..., None]).astype(v_all.dtype)
    o = jnp.einsum('bhtj,btjhe->bthe', p, vg)
    return o, lse.transpose(0, 2, 1)


def mix_dilations(results):
    dtype = results[0][0].dtype
    outs = jnp.stack([o for o, _ in results]).astype(jnp.float32)
    lses = jnp.stack([l for _, l in results])
    wts = jax.nn.softmax(lses, axis=0)
    o = jnp.einsum('cblh,cblhe->blhe', wts, outs)
    b, L, h, e = o.shape
    return o.reshape(b, L, h * e).astype(dtype)


def causal_dwconv(x_ext, w, bias):
    c = x_ext.shape[-1]
    rhs = jnp.transpose(w)[:, None, :].astype(x_ext.dtype)
    y = lax.conv_general_dilated(x_ext, rhs, (1,), 'VALID',
                                 dimension_numbers=('NWC', 'WIO', 'NWC'), feature_group_count=c)
    return y + bias


def ssd_scan(x, dt, A, Bm, Cm, D, h0, chunk):
    b, L, H, P = x.shape
    G, N = Bm.shape[2], Bm.shape[3]
    HG = H // G
    nc = L // chunk
    xc = x.astype(jnp.float32).reshape(b, nc, chunk, G, HG, P)
    dtc = dt.astype(jnp.float32).reshape(b, nc, chunk, G, HG)
    Bc = Bm.astype(jnp.float32).reshape(b, nc, chunk, G, N)
    Cc = Cm.astype(jnp.float32).reshape(b, nc, chunk, G, N)
    acs = jnp.cumsum(dtc * A.reshape(G, HG), axis=2)
    causal = jnp.tril(jnp.ones((chunk, chunk), dtype=bool))
    seg = acs[:, :, :, None] - acs[:, :, None, :]
    decay = jnp.exp(jnp.where(causal[:, :, None, None], seg, -jnp.inf))
    cb = jnp.einsum('bcign,bcjgn->bcijg', Cc, Bc)
    y = jnp.einsum('bcijgh,bcjghp->bcighp', cb[..., None] * decay * dtc[:, :, None], xc)
    to_end = jnp.exp(acs[:, :, -1:] - acs) * dtc
    states = jnp.einsum('bclgn,bclghp->bcghpn', Bc, xc * to_end[..., None])
    chunk_decay = jnp.exp(acs[:, :, -1])

    def step(hc, inp):
        st, dec = inp
        return hc * dec[..., None, None] + st, hc

    h_last, h_in = lax.scan(step, h0.astype(jnp.float32).reshape(b, G, HG, P, N),
                            (jnp.moveaxis(states, 1, 0), jnp.moveaxis(chunk_decay, 1, 0)))
    h_in = jnp.moveaxis(h_in, 0, 1)
    y = y + jnp.einsum('bclgn,bcghpn->bclghp', Cc, h_in) * jnp.exp(acs)[..., None]
    y = y + xc * D.astype(jnp.float32).reshape(G, HG)[:, :, None]
    return y.reshape(b, L, H, P), h_last.reshape(b, H, P, N)


def gated_rmsnorm(y, z, g):
    b, L = z.shape[:2]
    gs = D_SSM // N_SSM_GROUPS
    u = y.reshape(b, L, N_SSM_GROUPS, gs) * jax.nn.silu(z.astype(jnp.float32)).reshape(b, L, N_SSM_GROUPS, gs)
    u = u * lax.rsqrt(jnp.mean(u * u, axis=-1, keepdims=True) + RMS_EPS)
    return (u.reshape(b, L, D_SSM) * g.astype(jnp.float32)).astype(z.dtype)


def hybrid_mixer(h, w_in, conv_w, conv_b, dt_bias, a_log, d_skip, ssm_norm_g, w_out, slopes, hist):
    bsz, L, _ = h.shape
    proj = h @ w_in
    q = proj[..., Q_OFF:K_OFF].reshape(bsz, L, N_ATT_HEADS, HEAD_DIM)
    k = proj[..., K_OFF:V_OFF].reshape(bsz, L, N_ATT_HEADS, HEAD_DIM)
    v = proj[..., V_OFF:Z_OFF].reshape(bsz, L, N_ATT_HEADS, HEAD_DIM)
    z = proj[..., Z_OFF:XBC_OFF]
    xbc = proj[..., XBC_OFF:DT_OFF]
    dt_raw = proj[..., DT_OFF:N_IN]
    if hist is None:
        k_all, v_all = k, v
        conv_ext = jnp.pad(xbc, ((0, 0), (CONV_WIDTH - 1, 0), (0, 0)))
        h0 = jnp.zeros((bsz, N_SSM_HEADS, HEAD_DIM, D_STATE), jnp.float32)
        chunk = SSD_CHUNK
        att = [dilated_band_attention(q, k, v, w, d, slopes) for w, d in DILATED_CONFIGS]
    else:
        k_buf, v_buf, conv_buf, h0 = hist
        k_all = jnp.concatenate([k_buf, k], axis=1)
        v_all = jnp.concatenate([v_buf, v], axis=1)
        conv_ext = jnp.concatenate([conv_buf, xbc], axis=1)
        chunk = L
        att = [dilated_gather_attention(q, k_all, v_all, w, d, slopes) for w, d in DILATED_CONFIGS]
    att_out = mix_dilations(att)
    xbc_c = jax.nn.silu(causal_dwconv(conv_ext, conv_w, conv_b))
    gn = N_SSM_GROUPS * D_STATE
    xs = xbc_c[..., :D_SSM].reshape(bsz, L, N_SSM_HEADS, HEAD_DIM)
    Bm = xbc_c[..., D_SSM:D_SSM + gn].reshape(bsz, L, N_SSM_GROUPS, D_STATE)
    Cm = xbc_c[..., D_SSM + gn:].reshape(bsz, L, N_SSM_GROUPS, D_STATE)
    dt = jax.nn.softplus(dt_raw.astype(jnp.float32) + dt_bias.astype(jnp.float32))
    A = -jnp.exp(a_log.astype(jnp.float32))
    y, h_last = ssd_scan(xs, dt, A, Bm, Cm, d_skip, h0, chunk)
    ssm_out = gated_rmsnorm(y, z, ssm_norm_g)
    out = jnp.concatenate([att_out, ssm_out], axis=-1) @ w_out
    keep = max(0, k_all.shape[1] - MAX_WINDOW)
    new_state = (k_all[:, keep:], v_all[:, keep:], conv_ext[:, -(CONV_WIDTH - 1):], h_last.astype(h.dtype))
    return out, new_state


def peer_ffn(h, w_q, sub_keys, u_tab, v_tab):
    b, L, d = h.shape
    tok = h.reshape(b * L, d)
    n = tok.shape[0]
    q = (tok @ w_q).astype(jnp.float32).reshape(n, PEER_HEADS, 2, PEER_DKEY // 2)
    sk = sub_keys.astype(jnp.float32)
    s1 = jnp.einsum('nhk,ek->nhe', q[:, :, 0], sk[0])
    s2 = jnp.einsum('nhk,ek->nhe', q[:, :, 1], sk[1])
    v1, i1 = lax.top_k(s1, PEER_TOPK)
    v2, i2 = lax.top_k(s2, PEER_TOPK)
    cand = (v1[..., :, None] + v2[..., None, :]).reshape(n, PEER_HEADS, PEER_TOPK * PEER_TOPK)
    cidx = (i1[..., :, None] * PEER_NKEYS + i2[..., None, :]).reshape(n, PEER_HEADS, PEER_TOPK * PEER_TOPK)
    top_s, top_i = lax.top_k(cand, PEER_TOPK)
    expert = jnp.take_along_axis(cidx, top_i, axis=-1)
    gate = jax.nn.softmax(top_s, axis=-1).astype(h.dtype)
    n_pad = -(-n // PEER_BLOCK) * PEER_BLOCK
    pad = n_pad - n
    tok_b = jnp.pad(tok, ((0, pad), (0, 0))).reshape(-1, PEER_BLOCK, d)
    gate_b = jnp.pad(gate, ((0, pad), (0, 0), (0, 0))).reshape(-1, PEER_BLOCK, PEER_HEADS, PEER_TOPK)
    exp_b = jnp.pad(expert, ((0, pad), (0, 0), (0, 0))).reshape(-1, PEER_BLOCK, PEER_HEADS, PEER_TOPK)

    def run_block(args):
        t, g, e = args
        act = jax.nn.gelu(jnp.einsum('nd,nhkd->nhk', t, u_tab[e]), approximate=False)
        return jnp.einsum('nhk,nhkd->nd', g * act, v_tab[e])

    out = lax.map(run_block, (tok_b, gate_b, exp_b))
    return out.reshape(n_pad, d)[:n].reshape(b, L, d)


def setup_inputs(seed: int = 0) -> dict:
    key = jax.random.key(seed)
    ks = jax.random.split(key, 24)
    f32 = jnp.float32
    w_buf = min(MAX_WINDOW, PAST_LEN)

    def nrm(k, shape, scale):
        return jax.random.normal(k, shape, f32) * scale

    x_prompt = nrm(ks[0], (BATCH, SEQ, D_MODEL), 1.0)
    x_sample = nrm(ks[1], (DEC_BATCH, DEC_SEQ, D_MODEL), 1.0)
    cache_k = nrm(ks[2], (DEPTH, DEC_BATCH, w_buf, N_ATT_HEADS, HEAD_DIM), 1.0)
    cache_v = nrm(ks[3], (DEPTH, DEC_BATCH, w_buf, N_ATT_HEADS, HEAD_DIM), BETA)
    state_conv = nrm(ks[4], (DEPTH, DEC_BATCH, CONV_WIDTH - 1, CONV_DIM), 1.0)
    state_ssm = nrm(ks[5], (DEPTH, DEC_BATCH, N_SSM_HEADS, HEAD_DIM, D_STATE), 0.1)
    w_in = nrm(ks[6], (DEPTH, D_MODEL, N_IN), D_MODEL ** -0.5)
    w_in = w_in.at[:, :, V_OFF:Z_OFF].multiply(BETA)
    conv_w = nrm(ks[7], (DEPTH, CONV_DIM, CONV_WIDTH), CONV_WIDTH ** -0.5)
    conv_b = nrm(ks[8], (DEPTH, CONV_DIM), 0.02)
    dt0 = jnp.exp(jax.random.uniform(ks[9], (DEPTH, N_SSM_HEADS), f32, math.log(1e-3), math.log(1e-1)))
    dt_bias = dt0 + jnp.log(-jnp.expm1(-dt0))
    a_log = jnp.log(jax.random.uniform(ks[10], (DEPTH, N_SSM_HEADS), f32, 1.0, 16.0))
    d_skip = 1.0 + nrm(ks[11], (DEPTH, N_SSM_HEADS), 0.02)
    ssm_norm_g = 1.0 + nrm(ks[12], (DEPTH, D_SSM), 0.02)
    w_out = nrm(ks[13], (DEPTH, D_MIX, D_MODEL), D_MIX ** -0.5 * BETA)
    ln1_g = 1.0 + nrm(ks[14], (DEPTH, D_MODEL), 0.02)
    ln1_b = nrm(ks[15], (DEPTH, D_MODEL), 0.02)
    peer_wq = nrm(ks[16], (DEPTH, D_MODEL, PEER_HEADS * PEER_DKEY), D_MODEL ** -0.5)
    peer_keys = nrm(ks[17], (DEPTH, 2, PEER_NKEYS, PEER_DKEY // 2), (PEER_DKEY // 2) ** -0.5)
    peer_u = nrm(ks[18], (DEPTH, PEER_EXPERTS, D_MODEL), D_MODEL ** -0.5)
    peer_v = nrm(ks[19], (DEPTH, PEER_EXPERTS, D_MODEL), BETA * PEER_HEADS ** -0.5)
    ln2_g = 1.0 + nrm(ks[20], (DEPTH, D_MODEL), 0.02)
    ln2_b = nrm(ks[21], (DEPTH, D_MODEL), 0.02)
    return {'x_prompt': x_prompt, 'x_sample': x_sample, 'cache_k': cache_k, 'cache_v': cache_v,
            'state_conv': state_conv, 'state_ssm': state_ssm, 'w_in': w_in, 'conv_w': conv_w,
            'conv_b': conv_b, 'dt_bias': dt_bias, 'a_log': a_log, 'd_skip': d_skip,
            'ssm_norm_g': ssm_norm_g, 'w_out': w_out, 'ln1_g': ln1_g, 'ln1_b': ln1_b,
            'peer_wq': peer_wq, 'peer_keys': peer_keys, 'peer_u': peer_u, 'peer_v': peer_v,
            'ln2_g': ln2_g, 'ln2_b': ln2_b}


def reference(x_prompt, x_sample, cache_k, cache_v, state_conv, state_ssm, w_in, conv_w, conv_b,
              dt_bias, a_log, d_skip, ssm_norm_g, w_out, ln1_g, ln1_b, peer_wq, peer_keys,
              peer_u, peer_v, ln2_g, ln2_b):
    slopes = alibi_slopes()
    hp, hs = x_prompt, x_sample
    pk, pv, pc, ps = [], [], [], []
    sk, sv, sc, ss = [], [], [], []
    for l in range(DEPTH):
        mw = (w_in[l], conv_w[l], conv_b[l], dt_bias[l], a_log[l], d_skip[l], ssm_norm_g[l], w_out[l], slopes)
        mix_p, st_p = hybrid_mixer(hp, *mw, None)
        mix_s, st_s = hybrid_mixer(hs, *mw, (cache_k[l], cache_v[l], state_conv[l], state_ssm[l]))
        hp = layer_norm(ALPHA * hp + mix_p, ln1_g[l], ln1_b[l])
        hs = layer_norm(ALPHA * hs + mix_s, ln1_g[l], ln1_b[l])
        pw = (peer_wq[l], peer_keys[l], peer_u[l], peer_v[l])
        hp = layer_norm(ALPHA * hp + peer_ffn(hp, *pw), ln2_g[l], ln2_b[l])
        hs = layer_norm(ALPHA * hs + peer_ffn(hs, *pw), ln2_g[l], ln2_b[l])
        pk.append(st_p[0]); pv.append(st_p[1]); pc.append(st_p[2]); ps.append(st_p[3])
        sk.append(st_s[0]); sv.append(st_s[1]); sc.append(st_s[2]); ss.append(st_s[3])
    return (hp, hs, jnp.stack(pk), jnp.stack(pv), jnp.stack(pc), jnp.stack(ps),
            jnp.stack(sk), jnp.stack(sv), jnp.stack(sc), jnp.stack(ss))
```

```python
import functools
import math

import jax
import jax.numpy as jnp
from jax import lax
from jax.experimental import pallas as pl
from jax.experimental.pallas import tpu as pltpu

F32 = jnp.float32
BF16 = jnp.bfloat16

D_MODEL = 1024
DEPTH = 2
HEAD_DIM = 64
N_HEADS = 16
N_GROUPS = 2
D_STATE = 128
CONV_WIDTH = 4
CONV_DIM = D_MODEL + 2 * N_GROUPS * D_STATE
DILATED_CONFIGS = ((128, 1), (512, 4), (2048, 16))
MAX_WINDOW = 2048
BLK = 128
PEER_HEADS = 8
PEER_NKEYS = 128
PEER_TOPK = 16
LN_EPS = 1e-5
RMS_EPS = 1e-5
ALPHA = (2 * DEPTH) ** 0.25
DT_PAD = 128
PROJ_WIDTHS = (D_MODEL, D_MODEL, D_MODEL, D_MODEL, CONV_DIM, DT_PAD)
VMEM_LIMIT = 56 * 1024 * 1024
SLOPES = tuple(2.0 ** (-8.0 * (h + 1) / N_HEADS) for h in range(N_HEADS))
NEG_INF = float("-inf")


def _params(*sem):
    return pltpu.CompilerParams(dimension_semantics=sem, vmem_limit_bytes=VMEM_LIMIT)


def _layer_norm(x, g, b):
    mu = jnp.mean(x, axis=-1, keepdims=True)
    xc = x - mu
    var = jnp.mean(xc * xc, axis=-1, keepdims=True)
    return xc * lax.rsqrt(var + LN_EPS) * g + b


def _silu(x):
    return x * (1.0 / (1.0 + jnp.exp(-x)))


def _inproj_body(x_ref, w_ref, *out_refs):
    x = x_ref[...].astype(BF16)
    off = 0
    for ref in out_refs:
        n = ref.shape[-1]
        ref[...] = jnp.dot(x, w_ref[:, off:off + n], preferred_element_type=F32)
        off += n


def in_proj(x2d, w_pad):
    m = x2d.shape[0]
    tm = min(256, m)
    n_all = sum(PROJ_WIDTHS)
    return pl.pallas_call(
        _inproj_body,
        grid=(m // tm,),
        in_specs=[pl.BlockSpec((tm, D_MODEL), lambda i: (i, 0)),
                  pl.BlockSpec((D_MODEL, n_all), lambda i: (0, 0))],
        out_specs=[pl.BlockSpec((tm, n), lambda i: (i, 0)) for n in PROJ_WIDTHS],
        out_shape=[jax.ShapeDtypeStruct((m, n), F32) for n in PROJ_WIDTHS],
        compiler_params=_params("parallel"),
        name="in_proj",
    )(x2d, w_pad)


def _band_attn_body(q_ref, kp_ref, kc_ref, vp_ref, vc_ref, o_ref, lse_ref, *, dil):
    n = pl.program_id(2)
    q = (q_ref[0] * (HEAD_DIM ** -0.5)).astype(BF16)
    k2 = jnp.concatenate([kp_ref[0], kc_ref[0]], axis=0).astype(BF16)
    v2 = jnp.concatenate([vp_ref[0], vc_ref[0]], axis=0).astype(BF16)
    qi = lax.broadcasted_iota(jnp.int32, (BLK, 2 * BLK), 0)
    kj = lax.broadcasted_iota(jnp.int32, (BLK, 2 * BLK), 1)
    back = BLK + qi - kj
    valid = (back >= 0) & (back <= BLK) & ((kj >= BLK) | (n > 0))
    negdist = -(back * dil).astype(F32)
    lane = lax.broadcasted_iota(jnp.int32, (1, BLK), 1)
    lse_all = jnp.zeros((BLK, BLK), F32)
    for p in range(N_HEADS // 2):
        sl = slice(p * BLK, (p + 1) * BLK)
        qp, kp, vp = q[:, sl], k2[:, sl], v2[:, sl]
        acc = None
        for hh in range(2):
            h = 2 * p + hh
            hm = (lane < HEAD_DIM) if hh == 0 else (lane >= HEAD_DIM)
            qm = jnp.where(hm, qp, jnp.zeros_like(qp))
            s = lax.dot_general(qm, kp, (((1,), (1,)), ((), ())), preferred_element_type=F32)
            s = jnp.where(valid, s + SLOPES[h] * negdist, NEG_INF)
            mx = jnp.max(s, axis=1, keepdims=True)
            e = jnp.exp(s - mx)
            l = jnp.sum(e, axis=1, keepdims=True)
            vm = jnp.where(hm, vp, jnp.zeros_like(vp))
            pv = jnp.dot((e * (1.0 / l)).astype(BF16), vm, preferred_element_type=F32)
            acc = pv if acc is None else acc + pv
            lse_all = jnp.where(lane == h, mx + jnp.log(l), lse_all)
        o_ref[0, :, sl] = acc
    lse_ref[0] = lse_all


def band_attention(q, k, v, dil):
    b, s, d = q.shape
    sd = s // dil
    nblk = sd // BLK
    qv, kv, vv = (a.reshape(b, sd, dil * d) for a in (q, k, v))
    cur = lambda bi, r, n: (bi, n, r)
    prev = lambda bi, r, n: (bi, jnp.maximum(n - 1, 0), r)
    o, lse = pl.pallas_call(
        functools.partial(_band_attn_body, dil=dil),
        grid=(b, dil, nblk),
        in_specs=[pl.BlockSpec((1, BLK, d), cur),
                  pl.BlockSpec((1, BLK, d), prev), pl.BlockSpec((1, BLK, d), cur),
                  pl.BlockSpec((1, BLK, d), prev), pl.BlockSpec((1, BLK, d), cur)],
        out_specs=[pl.BlockSpec((1, BLK, d), cur), pl.BlockSpec((1, BLK, BLK), cur)],
        out_shape=[jax.ShapeDtypeStruct((b, sd, dil * d), F32),
                   jax.ShapeDtypeStruct((b, sd, dil * BLK), F32)],
        compiler_params=_params("parallel", "parallel", "arbitrary"),
        name=f"band_attn_d{dil}",
    )(qv, kv, kv, vv, vv)
    return o.reshape(b, s, d), lse.reshape(b, s, BLK)


def _cache_attn_body(q_ref, kn_ref, vn_ref, ck_ref, cv_ref, slope_ref, o_ref, *, t_new):
    w_buf = ck_ref.shape[1]
    tp = q_ref.shape[1]
    ncol = N_HEADS * tp
    hrow = lax.broadcasted_iota(jnp.int32, (N_HEADS, D_MODEL), 0)
    hlane = lax.broadcasted_iota(jnp.int32, (N_HEADS, D_MODEL), 1) // HEAD_DIM
    blockmask = hrow == hlane
    q = q_ref[0] * (HEAD_DIM ** -0.5)
    qbd = jnp.concatenate(
        [jnp.where(blockmask, jnp.broadcast_to(q[t:t + 1], (N_HEADS, D_MODEL)), 0.0) for t in range(tp)],
        axis=0).astype(BF16)
    nt = (((1,), (1,)), ((), ()))
    s_c = lax.dot_general(ck_ref[0].astype(BF16), qbd, nt, preferred_element_type=F32)
    s_n = lax.dot_general(kn_ref[0].astype(BF16), qbd, nt, preferred_element_type=F32)
    slope = slope_ref[...]

    def weights(s, key_pos):
        col_t = lax.broadcasted_iota(jnp.int32, s.shape, 1) // N_HEADS
        dist = (w_buf + col_t) - key_pos
        mult = jnp.zeros(s.shape, F32)
        for window, dil in DILATED_CONFIGS:
            hit = (dist >= 0) & (dist <= window) & ((dist & (dil - 1)) == 0)
            mult = mult + jnp.where(hit, 1.0, 0.0)
        sb = jnp.where(mult > 0.0, s - slope * dist.astype(F32), NEG_INF)
        return sb, mult

    sb_c, mult_c = weights(s_c, lax.broadcasted_iota(jnp.int32, s_c.shape, 0))
    sb_n, mult_n = weights(s_n, w_buf + lax.broadcasted_iota(jnp.int32, s_n.shape, 0))
    mx = jnp.maximum(jnp.max(sb_c, axis=0, keepdims=True), jnp.max(sb_n, axis=0, keepdims=True))
    p_c = mult_c * jnp.exp(sb_c - mx)
    p_n = mult_n * jnp.exp(sb_n - mx)
    inv = 1.0 / (jnp.sum(p_c, axis=0, keepdims=True) + jnp.sum(p_n, axis=0, keepdims=True))
    o_all = jnp.dot((p_c * inv).T.astype(BF16), cv_ref[0].astype(BF16), preferred_element_type=F32)
    o_all = o_all + jnp.dot((p_n * inv).T.astype(BF16), vn_ref[0].astype(BF16),
                            preferred_element_type=F32)
    rows = [jnp.sum(jnp.where(blockmask, o_all[t * N_HEADS:(t + 1) * N_HEADS], 0.0), axis=0, keepdims=True)
            for t in range(tp)]
    o_ref[0] = jnp.concatenate(rows, axis=0)
    del t_new


def cache_attention(q, k_new, v_new, cache_k, cache_v, t_new):
    b, tp, d = q.shape
    w_buf = cache_k.shape[1]
    slope_row = jnp.asarray([SLOPES[c % N_HEADS] for c in range(N_HEADS * tp)], F32)[None, :]
    row = lambda i: (i, 0, 0)
    return pl.pallas_call(
        functools.partial(_cache_attn_body, t_new=t_new),
        grid=(b,),
        in_specs=[pl.BlockSpec((1, tp, d), row), pl.BlockSpec((1, tp, d), row), pl.BlockSpec((1, tp, d), row),
                  pl.BlockSpec((1, w_buf, d), row), pl.BlockSpec((1, w_buf, d), row),
                  pl.BlockSpec((1, N_HEADS * tp), lambda i: (0, 0))],
        out_specs=pl.BlockSpec((1, tp, d), row),
        out_shape=jax.ShapeDtypeStruct((b, tp, d), F32),
        compiler_params=_params("parallel"),
        name="cache_attn",
    )(q, k_new, v_new, cache_k, cache_v, slope_row)


def _ssd_body(xbc_ref, z_ref, dt_ref, hist_ref, h0_ref, cw_ref, cb_ref, dtb_ref, alog_ref, dskip_ref, g_ref,
              y_ref, hl_ref, state_ref, tail_ref, *, valid):
    c = pl.program_id(1)

    @pl.when(c == 0)
    def _():
        state_ref[...] = h0_ref[0]
        tail_ref[...] = hist_ref[0]

    xbc = xbc_ref[0]
    ext = jnp.concatenate([tail_ref[...], xbc], axis=0)
    acc = jnp.broadcast_to(cb_ref[...], xbc.shape)
    for w in range(CONV_WIDTH):
        lo = 8 - (CONV_WIDTH - 1) + w
        acc = acc + ext[lo:lo + BLK] * cw_ref[w:w + 1, :]
    tail_ref[...] = xbc[BLK - 8:]
    xc = _silu(acc)
    xs = xc[:, :D_MODEL]
    gn = N_GROUPS * D_STATE

    row = lax.broadcasted_iota(jnp.int32, (BLK, DT_PAD), 0)
    dt = dt_ref[0] + dtb_ref[...]
    dt = jnp.maximum(dt, 0.0) + jnp.log(1.0 + jnp.exp(-jnp.abs(dt)))
    if valid < BLK:
        dt = jnp.where(row < valid, dt, 0.0)
    dta = dt * (-jnp.exp(alog_ref[...]))
    ri = lax.broadcasted_iota(jnp.int32, (BLK, BLK), 0)
    ci = lax.broadcasted_iota(jnp.int32, (BLK, BLK), 1)
    causal = ri >= ci
    tri = jnp.where(causal, 1.0, 0.0)
    acs = jnp.dot(tri, dta, preferred_element_type=F32, precision=lax.Precision.HIGHEST)
    acs_t = acs.T
    dt_t = dt.T
    lane = lax.broadcasted_iota(jnp.int32, (1, BLK), 1)
    lo_half = lane < HEAD_DIM
    nt = (((1,), (1,)), ((), ()))
    hg = N_HEADS // N_GROUPS
    for g in range(N_GROUPS):
        bg = xc[:, D_MODEL + g * D_STATE:D_MODEL + (g + 1) * D_STATE].astype(BF16)
        cg = xc[:, D_MODEL + gn + g * D_STATE:D_MODEL + gn + (g + 1) * D_STATE].astype(BF16)
        cb = lax.dot_general(cg, bg, nt, preferred_element_type=F32)
        for pp in range(hg // 2):
            p = g * (hg // 2) + pp
            sl = slice(p * BLK, (p + 1) * BLK)
            x_pair = xs[:, sl]
            x_pair_b = x_pair.astype(BF16)
            y_pair = None
            dec_in = None
            to_end = None
            dec_out = None
            dsk = None
            for hh in range(2):
                h = 2 * p + hh
                hm = lo_half if hh == 0 else jnp.logical_not(lo_half)
                a_col = acs[:, h:h + 1]
                seg = a_col - acs_t[h:h + 1, :]
                lmat = jnp.exp(jnp.where(causal, seg, NEG_INF))
                mh = (cb * lmat * dt_t[h:h + 1, :]).astype(BF16)
                xm = jnp.where(hm, x_pair_b, jnp.zeros_like(x_pair_b))
                yh = jnp.dot(mh, xm, preferred_element_type=F32)
                y_pair = yh if y_pair is None else y_pair + yh
                a_last = acs[BLK - 1:BLK, h:h + 1]
                e_in = jnp.exp(a_col)
                e_end = jnp.exp(a_last - a_col) * dt[:, h:h + 1]
                e_out = jnp.exp(a_last)
                d_h = dskip_ref[h]
                if hh == 0:
                    dec_in, to_end, dec_out, dsk = e_in, e_end, e_out, d_h
                else:
                    dec_in = jnp.where(lo_half, dec_in, e_in)
                    to_end = jnp.where(lo_half, to_end, e_end)
                    dec_out = jnp.where(lo_half, dec_out, e_out)
                    dsk = jnp.where(lo_half, dsk, d_h)
            h_in = state_ref[sl, :]
            y_off = lax.dot_general(cg, h_in.astype(BF16), nt, preferred_element_type=F32)
            y_pair = y_pair + y_off * dec_in + x_pair * dsk
            xw_t = (x_pair * to_end).T.astype(BF16)
            st = jnp.dot(xw_t, bg, preferred_element_type=F32)
            state_ref[sl, :] = h_in * dec_out.T + st
            y_ref[0, :, sl] = y_pair

    u = y_ref[0] * _silu(z_ref[0])
    gs = D_MODEL // N_GROUPS
    parts = []
    for g in range(N_GROUPS):
        ug = u[:, g * gs:(g + 1) * gs]
        parts.append(ug * lax.rsqrt(jnp.mean(ug * ug, axis=-1, keepdims=True) + RMS_EPS))
    y_ref[0] = jnp.concatenate(parts, axis=1) * g_ref[...]

    @pl.when(c == pl.num_programs(1) - 1)
    def _():
        hl_ref[0] = state_ref[...]


def ssd_mixer(xbc, z, dt_raw, hist, h0, conv_w_t, conv_b, dt_bias, a_log, d_skip, norm_g, valid):
    b, l, _ = xbc.shape
    nc = l // BLK
    hp = N_HEADS * HEAD_DIM
    tok = lambda bi, c: (bi, c, 0)
    seq = lambda bi, c: (bi, 0, 0)
    const = lambda bi, c: (0, 0)
    return pl.pallas_call(
        functools.partial(_ssd_body, valid=valid),
        grid=(b, nc),
        in_specs=[pl.BlockSpec((1, BLK, CONV_DIM), tok), pl.BlockSpec((1, BLK, D_MODEL), tok),
                  pl.BlockSpec((1, BLK, DT_PAD), tok), pl.BlockSpec((1, 8, CONV_DIM), seq),
                  pl.BlockSpec((1, hp, D_STATE), seq),
                  pl.BlockSpec((CONV_WIDTH, CONV_DIM), const), pl.BlockSpec((1, CONV_DIM), const),
                  pl.BlockSpec((1, DT_PAD), const), pl.BlockSpec((1, DT_PAD), const),
                  pl.BlockSpec(memory_space=pltpu.SMEM), pl.BlockSpec((1, D_MODEL), const)],
        out_specs=[pl.BlockSpec((1, BLK, D_MODEL), tok), pl.BlockSpec((1, hp, D_STATE), seq)],
        out_shape=[jax.ShapeDtypeStruct((b, l, D_MODEL), F32), jax.ShapeDtypeStruct((b, hp, D_STATE), F32)],
        scratch_shapes=[pltpu.VMEM((hp, D_STATE), F32), pltpu.VMEM((8, CONV_DIM), F32)],
        compiler_params=_params("parallel", "arbitrary"),
        name="ssd_mixer",
    )(xbc, z, dt_raw, hist, h0, conv_w_t, conv_b, dt_bias, a_log, d_skip, norm_g)


def _outproj_body(*refs, n_cfg):
    o_refs = refs[:n_cfg]
    lse_refs = refs[n_cfg:2 * n_cfg] if n_cfg > 1 else ()
    ssm_ref, h_ref, w_ref, g_ref, b_ref, exp_ref, y_ref = refs[-7:]
    if n_cfg == 1:
        att = o_refs[0][...]
    else:
        lses = [r[...] for r in lse_refs]
        mx = functools.reduce(jnp.maximum, lses)
        es = [jnp.exp(l - mx) for l in lses]
        inv = 1.0 / functools.reduce(lambda a, c: a + c, es)
        att = None
        for o_ref, e in zip(o_refs, es):
            w = e * inv
            hi = w.astype(BF16)
            lo = (w - hi.astype(F32)).astype(BF16)
            wexp = (jnp.dot(hi, exp_ref[...], preferred_element_type=F32)
                    + jnp.dot(lo, exp_ref[...], preferred_element_type=F32))
            term = wexp * o_ref[...]
            att = term if att is None else att + term
    out = jnp.dot(att.astype(BF16), w_ref[:D_MODEL, :], preferred_element_type=F32)
    out = out + jnp.dot(ssm_ref[...].astype(BF16), w_ref[D_MODEL:, :], preferred_element_type=F32)
    y_ref[...] = _layer_norm(ALPHA * h_ref[...] + out, g_ref[...], b_ref[...])


def out_proj(att_parts, lse_parts, ssm, h, w_out_b, ln_g, ln_b, head_expand):
    m = h.shape[0]
    tm = min(256, m)
    n_cfg = len(att_parts)
    tok = lambda i: (i, 0)
    const = lambda i: (0, 0)
    ins = list(att_parts) + (list(lse_parts) if n_cfg > 1 else [])
    specs = [pl.BlockSpec((tm, D_MODEL), tok)] * n_cfg + ([pl.BlockSpec((tm, BLK), tok)] * n_cfg if n_cfg > 1 else [])
    return pl.pallas_call(
        functools.partial(_outproj_body, n_cfg=n_cfg),
        grid=(m // tm,),
        in_specs=specs + [pl.BlockSpec((tm, D_MODEL), tok), pl.BlockSpec((tm, D_MODEL), tok),
                          pl.BlockSpec((2 * D_MODEL, D_MODEL), const), pl.BlockSpec((1, D_MODEL), const),
                          pl.BlockSpec((1, D_MODEL), const), pl.BlockSpec((BLK, D_MODEL), const)],
        out_specs=pl.BlockSpec((tm, D_MODEL), tok),
        out_shape=jax.ShapeDtypeStruct((m, D_MODEL), F32),
        compiler_params=_params("parallel"),
        name=f"out_proj_c{n_cfg}",
    )(*ins, ssm, h, w_out_b, ln_g, ln_b, head_expand)


def _fold_body(sk_ref, wq_ref, o_ref):
    o_ref[0] = lax.dot_general(sk_ref[0], wq_ref[...], (((1,), (1,)), ((), ())),
                               preferred_element_type=F32, precision=lax.Precision.HIGHEST)


def fold_scores(peer_wq, peer_keys):
    dk = peer_keys.shape[-1]
    nblk = PEER_HEADS * 2
    folded = pl.pallas_call(
        _fold_body,
        grid=(nblk,),
        in_specs=[pl.BlockSpec((1, PEER_NKEYS, dk), lambda i: (i % 2, 0, 0)),
                  pl.BlockSpec((D_MODEL, dk), lambda i: (0, i))],
        out_specs=pl.BlockSpec((1, PEER_NKEYS, D_MODEL), lambda i: (i, 0, 0)),
        out_shape=jax.ShapeDtypeStruct((nblk, PEER_NKEYS, D_MODEL), F32),
        compiler_params=_params("parallel"),
        name="peer_fold",
    )(peer_keys, peer_wq)
    folded = folded.reshape(PEER_HEADS, 2, PEER_NKEYS, D_MODEL).transpose(1, 2, 0, 3)
    return folded.reshape(2 * PEER_NKEYS * PEER_HEADS, D_MODEL)


_CANDS = tuple((k, l) for k in range(PEER_TOPK) for l in range(PEER_TOPK) if (k + 1) * (l + 1) <= PEER_TOPK)


def _topk_extract(score_ref, work_ref, rank_ref, vals_ref):
    shape = work_ref.shape
    eio = lax.broadcasted_iota(jnp.int32, shape, 0)
    work_ref[...] = score_ref[...]
    rank_ref[...] = jnp.full(shape, 99.0, F32)

    def body(t, carry):
        sw = work_ref[...]
        m = jnp.max(sw, axis=0)
        idx = jnp.min(jnp.where(sw == m[None], eio, PEER_NKEYS), axis=0)
        sel = eio == idx[None]
        work_ref[...] = jnp.where(sel, NEG_INF, sw)
        rank_ref[...] = jnp.where(sel, t.astype(F32), rank_ref[...])
        vals_ref[t] = m
        return carry

    lax.fori_loop(0, PEER_TOPK, body, 0)


def _route_body(h_ref, ws_ref, cmap_ref, e1_ref, r2_ref, w2_ref,
                s_scr, work_scr, rank1_scr, rank2_scr, a_scr, b_scr, tmp_scr):
    tl = s_scr.shape[-1]
    nk, nh = PEER_NKEYS, PEER_HEADS
    s_t = lax.dot_general(ws_ref[...], h_ref[...].astype(BF16), (((1,), (1,)), ((), ())),
                          preferred_element_type=F32)
    s_scr[...] = s_t.reshape(2, nk, nh, tl)
    _topk_extract(s_scr.at[0], work_scr, rank1_scr, a_scr)
    _topk_extract(s_scr.at[1], work_scr, rank2_scr, b_scr)

    cvals = [a_scr[k] + b_scr[l] for k, l in _CANDS]
    nc = len(_CANDS)
    after = [None] * nc
    before = [None] * nc
    for x in range(nc):
        for y in range(x + 1, nc):
            gt = jnp.where(cvals[y] > cvals[x], 1.0, 0.0)
            after[x] = gt if after[x] is None else after[x] + gt
            before[y] = gt if before[y] is None else before[y] + gt
    z = None
    counts = [None] * PEER_TOPK
    for x, (k, l) in enumerate(_CANDS):
        ahead = float(x)
        if after[x] is not None:
            ahead = ahead + after[x]
        if before[x] is not None:
            ahead = ahead - before[x]
        chosen = ahead < float(PEER_TOPK)
        one = jnp.where(chosen, 1.0, 0.0)
        counts[k] = one if counts[k] is None else counts[k] + one
        ez = jnp.where(chosen, jnp.exp(cvals[x] - cvals[0]), 0.0)
        z = ez if z is None else z + ez
    inv_z = 1.0 / z

    rank1 = rank1_scr[...]
    cmap = jnp.zeros(rank1.shape, F32)
    for k in range(PEER_TOPK):
        cmap = jnp.where(rank1 == float(k), counts[k][None], cmap)
    cmap_ref[...] = cmap
    e1_ref[...] = jnp.where(rank1 < float(PEER_TOPK), jnp.exp(s_scr[0] - a_scr[0][None]), 0.0)
    rank2 = rank2_scr[...]
    w2 = jnp.where(rank2 < float(PEER_TOPK), jnp.exp(s_scr[1] - b_scr[0][None]) * inv_z[None], 0.0)
    tmp_scr[...] = rank2.reshape(nk * nh, tl)
    for h in range(nh):
        r2_ref[h] = tmp_scr[pl.ds(h, nk, stride=nh), :].astype(BF16)
    tmp_scr[...] = w2.reshape(nk * nh, tl)
    for h in range(nh):
        w2_ref[h] = tmp_scr[pl.ds(h, nk, stride=nh), :].astype(BF16)


def peer_route(h2d, ws_t):
    m = h2d.shape[0]
    tl = 128
    nk, nh = PEER_NKEYS, PEER_HEADS
    tok3 = lambda i: (0, 0, i)
    return pl.pallas_call(
        _route_body,
        grid=(m // tl,),
        in_specs=[pl.BlockSpec((tl, D_MODEL), lambda i: (i, 0)),
                  pl.BlockSpec((2 * nk * nh, D_MODEL), lambda i: (0, 0))],
        out_specs=[pl.BlockSpec((nk, nh, tl), tok3), pl.BlockSpec((nk, nh, tl), tok3),
                   pl.BlockSpec((nh, nk, tl), tok3), pl.BlockSpec((nh, nk, tl), tok3)],
        out_shape=[jax.ShapeDtypeStruct((nk, nh, m), F32), jax.ShapeDtypeStruct((nk, nh, m), F32),
                   jax.ShapeDtypeStruct((nh, nk, m), BF16), jax.ShapeDtypeStruct((nh, nk, m), BF16)],
        scratch_shapes=[pltpu.VMEM((2, nk, nh, tl), F32), pltpu.VMEM((nk, nh, tl), F32),
                        pltpu.VMEM((nk, nh, tl), F32), pltpu.VMEM((nk, nh, tl), F32),
                        pltpu.VMEM((PEER_TOPK, nh, tl), F32), pltpu.VMEM((PEER_TOPK, nh, tl), F32),
                        pltpu.VMEM((nk * nh, tl), F32)],
        compiler_params=_params("parallel"),
        name="peer_route",
    )(h2d, ws_t)


def _experts_body(h_ref, u_ref, vt_ref, cmap_ref, e1_ref, r2_ref, w2_ref, g_ref, b_ref, y_ref,
                  acc_ref, hid_ref, *, rows_per_step):
    j = pl.program_id(1)
    tm = h_ref.shape[0]

    @pl.when(j == 0)
    def _():
        acc_ref[...] = jnp.zeros_like(acc_ref)

    hb = h_ref[...].astype(BF16)
    a_t = lax.dot_general(u_ref[...], hb, (((1,), (1,)), ((), ())), preferred_element_type=F32)
    for ib in range(rows_per_step):
        i = j * rows_per_step + ib
        gate = jnp.zeros((PEER_NKEYS, tm), BF16)
        for h in range(PEER_HEADS):
            cm = jnp.broadcast_to(cmap_ref[i, h:h + 1, :].astype(BF16), (PEER_NKEYS, tm))
            e1 = jnp.broadcast_to(e1_ref[i, h:h + 1, :].astype(BF16), (PEER_NKEYS, tm))
            w2 = w2_ref[h]
            gate = gate + jnp.where(r2_ref[h] < cm, w2, jnp.zeros_like(w2)) * e1
        a = a_t[ib * PEER_NKEYS:(ib + 1) * PEER_NKEYS]
        act = 0.5 * a * (1.0 + lax.erf(a * (2.0 ** -0.5)))
        hid_ref[ib * PEER_NKEYS:(ib + 1) * PEER_NKEYS, :] = gate * act.astype(BF16)
    acc_ref[...] += jnp.dot(vt_ref[...], hid_ref[...], preferred_element_type=F32)

    @pl.when(j == pl.num_programs(1) - 1)
    def _():
        y_ref[...] = _layer_norm(ALPHA * h_ref[...] + acc_ref[...].T, g_ref[...], b_ref[...])


def peer_experts(h2d, u_b, vt_b, cmap, e1map, rank2, w2, ln_g, ln_b):
    m = h2d.shape[0]
    tm = min(512, m)
    te = 1024
    n_exp = u_b.shape[0]
    rows = te // PEER_NKEYS
    nk, nh = PEER_NKEYS, PEER_HEADS
    tok = lambda i, j: (i, 0)
    tok3 = lambda i, j: (0, 0, i)
    const = lambda i, j: (0, 0)
    return pl.pallas_call(
        functools.partial(_experts_body, rows_per_step=rows),
        grid=(m // tm, n_exp // te),
        in_specs=[pl.BlockSpec((tm, D_MODEL), tok),
                  pl.BlockSpec((te, D_MODEL), lambda i, j: (j, 0)),
                  pl.BlockSpec((D_MODEL, te), lambda i, j: (0, j)),
                  pl.BlockSpec((nk, nh, tm), tok3), pl.BlockSpec((nk, nh, tm), tok3),
                  pl.BlockSpec((nh, nk, tm), tok3), pl.BlockSpec((nh, nk, tm), tok3),
                  pl.BlockSpec((1, D_MODEL), const), pl.BlockSpec((1, D_MODEL), const)],
        out_specs=pl.BlockSpec((tm, D_MODEL), tok),
        out_shape=jax.ShapeDtypeStruct((m, D_MODEL), F32),
        scratch_shapes=[pltpu.VMEM((D_MODEL, tm), F32), pltpu.VMEM((te, tm), BF16)],
        compiler_params=_params("parallel", "arbitrary"),
        name="peer_experts",
    )(h2d, u_b, vt_b, cmap, e1map, rank2, w2, ln_g, ln_b)


def peer_layer(h2d, ws_t, u_b, vt_b, ln_g, ln_b):
    cmap, e1map, rank2, w2 = peer_route(h2d, ws_t)
    return peer_experts(h2d, u_b, vt_b, cmap, e1map, rank2, w2, ln_g, ln_b)


def kernel(x_prompt, x_sample, cache_k, cache_v, state_conv, state_ssm, w_in, conv_w, conv_b, dt_bias, a_log,
           d_skip, ssm_norm_g, w_out, ln1_g, ln1_b, peer_wq, peer_keys, peer_u, peer_v, ln2_g, ln2_b):
    bp, sp, d = x_prompt.shape
    bs, ts, _ = x_sample.shape
    w_buf = cache_k.shape[2]
    hp = N_HEADS * HEAD_DIM
    tq = 8
    hpv = x_prompt.reshape(bp * sp, d)
    hsv = x_sample.reshape(bs * ts, d)
    lane_head = jnp.arange(D_MODEL)[None, :] // HEAD_DIM
    head_expand = (jnp.arange(BLK)[:, None] == lane_head).astype(BF16)
    pad_lanes = lambda a: jnp.pad(a[None, :], ((0, 0), (0, DT_PAD - a.shape[0])))
    outs = {n: [] for n in ("pk", "pv", "pc", "ps", "sk", "sv", "sc", "ss")}

    for l in range(DEPTH):
        w_pad = jnp.pad(w_in[l], ((0, 0), (0, DT_PAD - N_HEADS))).astype(BF16)
        w_out_b = w_out[l].astype(BF16)
        cw_t = conv_w[l].T
        cb = conv_b[l][None, :]
        dtb, alog = pad_lanes(dt_bias[l]), pad_lanes(a_log[l])
        ng = ssm_norm_g[l][None, :]
        g1, b1, g2, b2 = ln1_g[l][None, :], ln1_b[l][None, :], ln2_g[l][None, :], ln2_b[l][None, :]
        ws_t = fold_scores(peer_wq[l], peer_keys[l]).astype(BF16)
        u_b = peer_u[l].astype(BF16)
        vt_b = peer_v[l].T.astype(BF16)

        q, k, v, z, xbc, dtr = in_proj(hpv, w_pad)
        q3, k3, v3 = (a.reshape(bp, sp, d) for a in (q, k, v))
        att = [band_attention(q3, k3, v3, dil) for _, dil in DILATED_CONFIGS]
        y, h_last = ssd_mixer(xbc.reshape(bp, sp, CONV_DIM), z.reshape(bp, sp, d), dtr.reshape(bp, sp, DT_PAD),
                              jnp.zeros((bp, 8, CONV_DIM), F32), jnp.zeros((bp, hp, D_STATE), F32),
                              cw_t, cb, dtb, alog, d_skip[l], ng, valid=BLK)
        hpv = out_proj([o.reshape(bp * sp, d) for o, _ in att], [s.reshape(bp * sp, BLK) for _, s in att],
                       y.reshape(bp * sp, d), hpv, w_out_b, g1, b1, head_expand)
        hpv = peer_layer(hpv, ws_t, u_b, vt_b, g2, b2)
        keep = max(0, sp - MAX_WINDOW)
        outs["pk"].append(k3[:, keep:].reshape(bp, sp - keep, N_HEADS, HEAD_DIM))
        outs["pv"].append(v3[:, keep:].reshape(bp, sp - keep, N_HEADS, HEAD_DIM))
        outs["pc"].append(xbc.reshape(bp, sp, CONV_DIM)[:, sp - (CONV_WIDTH - 1):])
        outs["ps"].append(h_last.reshape(bp, N_HEADS, HEAD_DIM, D_STATE))

        q, k, v, z, xbc, dtr = in_proj(hsv, w_pad)
        pad_t = lambda a, n: jnp.pad(a.reshape(bs, ts, a.shape[-1]), ((0, 0), (0, n - ts), (0, 0)))
        ck = cache_k[l].reshape(bs, w_buf, d)
        cv = cache_v[l].reshape(bs, w_buf, d)
        att_s = cache_attention(pad_t(q, tq), pad_t(k, tq), pad_t(v, tq), ck, cv, ts)[:, :ts].reshape(bs * ts, d)
        hist = jnp.pad(state_conv[l], ((0, 0), (8 - (CONV_WIDTH - 1), 0), (0, 0)))
        y, h_last = ssd_mixer(pad_t(xbc, BLK), pad_t(z, BLK), pad_t(dtr, BLK), hist,
                              state_ssm[l].reshape(bs, hp, D_STATE), cw_t, cb, dtb, alog, d_skip[l], ng, valid=ts)
        y = y[:, :ts].reshape(bs * ts, d)
        hsv = out_proj([att_s], [], y, hsv, w_out_b, g1, b1, head_expand)
        hsv = peer_layer(hsv, ws_t, u_b, vt_b, g2, b2)
        k_all = jnp.concatenate([ck, k.reshape(bs, ts, d)], axis=1)
        v_all = jnp.concatenate([cv, v.reshape(bs, ts, d)], axis=1)
        keep = max(0, w_buf + ts - MAX_WINDOW)
        outs["sk"].append(k_all[:, keep:].reshape(bs, -1, N_HEADS, HEAD_DIM))
        outs["sv"].append(v_all[:, keep:].reshape(bs, -1, N_HEADS, HEAD_DIM))
        conv_ext = jnp.concatenate([state_conv[l], xbc.reshape(bs, ts, CONV_DIM)], axis=1)
        outs["sc"].append(conv_ext[:, -(CONV_WIDTH - 1):])
        outs["ss"].append(h_last.reshape(bs, N_HEADS, HEAD_DIM, D_STATE))

    st = {n: jnp.stack(v) for n, v in outs.items()}
    return (hpv.reshape(bp, sp, d), hsv.reshape(bs, ts, d), st["pk"], st["pv"], st["pc"], st["ps"],
            st["sk"], st["sv"], st["sc"], st["ss"])
```

```python
import functools
import math

import jax
import jax.numpy as jnp
from jax import lax
from jax.experimental import pallas as pl
from jax.experimental.pallas import tpu as pltpu

F32 = jnp.float32
BF16 = jnp.bfloat16

D_MODEL = 1024
DEPTH = 2
HEAD_DIM = 64
N_HEADS = 16
N_GROUPS = 2
D_STATE = 128
CONV_WIDTH = 4
CONV_DIM = D_MODEL + 2 * N_GROUPS * D_STATE
DILATED_CONFIGS = ((128, 1), (512, 4), (2048, 16))
MAX_WINDOW = 2048
BLK = 128
PEER_HEADS = 8
PEER_NKEYS = 128
PEER_TOPK = 16
LN_EPS = 1e-5
RMS_EPS = 1e-5
ALPHA = (2 * DEPTH) ** 0.25
DT_PAD = 128
PROJ_WIDTHS = (D_MODEL, D_MODEL, D_MODEL, D_MODEL, CONV_DIM, DT_PAD)
VMEM_LIMIT = 56 * 1024 * 1024
SLOPES = tuple(2.0 ** (-8.0 * (h + 1) / N_HEADS) for h in range(N_HEADS))
NEG_INF = float("-inf")


def _params(*sem):
    return pltpu.CompilerParams(dimension_semantics=sem, vmem_limit_bytes=VMEM_LIMIT)


def _layer_norm(x, g, b):
    mu = jnp.mean(x, axis=-1, keepdims=True)
    xc = x - mu
    var = jnp.mean(xc * xc, axis=-1, keepdims=True)
    return xc * lax.rsqrt(var + LN_EPS) * g + b


def _silu(x):
    return x * (1.0 / (1.0 + jnp.exp(-x)))


def _inproj_body(x_ref, w_ref, *out_refs):
    x = x_ref[...].astype(BF16)
    off = 0
    for ref in out_refs:
        n = ref.shape[-1]
        ref[...] = jnp.dot(x, w_ref[:, off:off + n], preferred_element_type=F32)
        off += n


def in_proj(x2d, w_pad):
    m = x2d.shape[0]
    tm = min(256, m)
    n_all = sum(PROJ_WIDTHS)
    return pl.pallas_call(
        _inproj_body,
        grid=(m // tm,),
        in_specs=[pl.BlockSpec((tm, D_MODEL), lambda i: (i, 0)),
                  pl.BlockSpec((D_MODEL, n_all), lambda i: (0, 0))],
        out_specs=[pl.BlockSpec((tm, n), lambda i: (i, 0)) for n in PROJ_WIDTHS],
        out_shape=[jax.ShapeDtypeStruct((m, n), F32) for n in PROJ_WIDTHS],
        compiler_params=_params("parallel"),
        name="in_proj",
    )(x2d, w_pad)


def _band_attn_body(q_ref, kp_ref, kc_ref, vp_ref, vc_ref, o_ref, lse_ref, *, dil):
    n = pl.program_id(2)
    q = (q_ref[0] * (HEAD_DIM ** -0.5)).astype(BF16)
    k2 = jnp.concatenate([kp_ref[0], kc_ref[0]], axis=0).astype(BF16)
    v2 = jnp.concatenate([vp_ref[0], vc_ref[0]], axis=0).astype(BF16)
    qi = lax.broadcasted_iota(jnp.int32, (BLK, 2 * BLK), 0)
    kj = lax.broadcasted_iota(jnp.int32, (BLK, 2 * BLK), 1)
    back = BLK + qi - kj
    valid = (back >= 0) & (back <= BLK) & ((kj >= BLK) | (n > 0))
    negdist = -(back * dil).astype(F32)
    lane = lax.broadcasted_iota(jnp.int32, (1, BLK), 1)
    lse_all = jnp.zeros((BLK, BLK), F32)
    for p in range(N_HEADS // 2):
        sl = slice(p * BLK, (p + 1) * BLK)
        qp, kp, vp = q[:, sl], k2[:, sl], v2[:, sl]
        acc = None
        for hh in range(2):
            h = 2 * p + hh
            hm = (lane < HEAD_DIM) if hh == 0 else (lane >= HEAD_DIM)
            qm = jnp.where(hm, qp, jnp.zeros_like(qp))
            s = lax.dot_general(qm, kp, (((1,), (1,)), ((), ())), preferred_element_type=F32)
            s = jnp.where(valid, s + SLOPES[h] * negdist, NEG_INF)
            mx = jnp.max(s, axis=1, keepdims=True)
            e = jnp.exp(s - mx)
            l = jnp.sum(e, axis=1, keepdims=True)
            vm = jnp.where(hm, vp, jnp.zeros_like(vp))
            pv = jnp.dot((e * (1.0 / l)).astype(BF16), vm, preferred_element_type=F32)
            acc = pv if acc is None else acc + pv
            lse_all = jnp.where(lane == h, mx + jnp.log(l), lse_all)
        o_ref[0, :, sl] = acc
    lse_ref[0] = lse_all


def _band_attn_strided_body(slope_ref, q_ref, kp_ref, kc_ref, vp_ref, vc_ref, o_ref, lse_ref, *, dil):
    n = pl.program_id(1)
    p = pl.program_id(2)

    @pl.when(p == 0)
    def _():
        lse_ref[...] = jnp.zeros_like(lse_ref)

    qi = lax.broadcasted_iota(jnp.int32, (BLK, 2 * BLK), 0)
    kj = lax.broadcasted_iota(jnp.int32, (BLK, 2 * BLK), 1)
    back = BLK + qi - kj
    valid = (back >= 0) & (back <= BLK) & ((kj >= BLK) | (n > 0))
    negdist = -(back * dil).astype(F32)
    lane = lax.broadcasted_iota(jnp.int32, (1, BLK), 1)
    for r in range(dil):
        rows = pl.ds(r, BLK, stride=dil)
        qp = (q_ref[0, rows, :] * (HEAD_DIM ** -0.5)).astype(BF16)
        kp = jnp.concatenate([kp_ref[0, rows, :], kc_ref[0, rows, :]], axis=0).astype(BF16)
        vp = jnp.concatenate([vp_ref[0, rows, :], vc_ref[0, rows, :]], axis=0).astype(BF16)
        lse_rows = lse_ref[0, rows, :]
        acc = None
        for hh in range(2):
            h = 2 * p + hh
            hm = (lane < HEAD_DIM) if hh == 0 else (lane >= HEAD_DIM)
            qm = jnp.where(hm, qp, jnp.zeros_like(qp))
            s = lax.dot_general(qm, kp, (((1,), (1,)), ((), ())), preferred_element_type=F32)
            s = jnp.where(valid, s + slope_ref[h] * negdist, NEG_INF)
            mx = jnp.max(s, axis=1, keepdims=True)
            e = jnp.exp(s - mx)
            l = jnp.sum(e, axis=1, keepdims=True)
            vm = jnp.where(hm, vp, jnp.zeros_like(vp))
            pv = jnp.dot((e * (1.0 / l)).astype(BF16), vm, preferred_element_type=F32)
            acc = pv if acc is None else acc + pv
            lse_rows = jnp.where(lane == h, mx + jnp.log(l), lse_rows)
        o_ref[0, rows, :] = acc
        lse_ref[0, rows, :] = lse_rows


def band_attention(q, k, v, dil):
    b, s, d = q.shape
    if dil == 1:
        nblk = s // BLK
        cur = lambda bi, r, n: (bi, n, r)
        prev = lambda bi, r, n: (bi, jnp.maximum(n - 1, 0), r)
        return pl.pallas_call(
            functools.partial(_band_attn_body, dil=dil),
            grid=(b, 1, nblk),
            in_specs=[pl.BlockSpec((1, BLK, d), cur),
                      pl.BlockSpec((1, BLK, d), prev), pl.BlockSpec((1, BLK, d), cur),
                      pl.BlockSpec((1, BLK, d), prev), pl.BlockSpec((1, BLK, d), cur)],
            out_specs=[pl.BlockSpec((1, BLK, d), cur), pl.BlockSpec((1, BLK, BLK), cur)],
            out_shape=[jax.ShapeDtypeStruct((b, s, d), F32), jax.ShapeDtypeStruct((b, s, BLK), F32)],
            compiler_params=_params("parallel", "parallel", "arbitrary"),
            name="band_attn_d1",
        )(q, k, k, v, v)
    unit = BLK * dil
    cur = lambda bi, n, p: (bi, n, p)
    prev = lambda bi, n, p: (bi, jnp.maximum(n - 1, 0), p)
    return pl.pallas_call(
        functools.partial(_band_attn_strided_body, dil=dil),
        grid=(b, s // unit, d // BLK),
        in_specs=[pl.BlockSpec(memory_space=pltpu.SMEM),
                  pl.BlockSpec((1, unit, BLK), cur),
                  pl.BlockSpec((1, unit, BLK), prev), pl.BlockSpec((1, unit, BLK), cur),
                  pl.BlockSpec((1, unit, BLK), prev), pl.BlockSpec((1, unit, BLK), cur)],
        out_specs=[pl.BlockSpec((1, unit, BLK), cur), pl.BlockSpec((1, unit, BLK), lambda bi, n, p: (bi, n, 0))],
        out_shape=[jax.ShapeDtypeStruct((b, s, d), F32), jax.ShapeDtypeStruct((b, s, BLK), F32)],
        compiler_params=_params("parallel", "parallel", "arbitrary"),
        name=f"band_attn_d{dil}",
    )(jnp.asarray(SLOPES, F32), q, k, k, v, v)


def _cache_attn_body(q_ref, kn_ref, vn_ref, ck_ref, cv_ref, slope_ref, o_ref, *, t_new):
    w_buf = ck_ref.shape[1]
    tp = q_ref.shape[1]
    ncol = N_HEADS * tp
    hrow = lax.broadcasted_iota(jnp.int32, (N_HEADS, D_MODEL), 0)
    hlane = lax.broadcasted_iota(jnp.int32, (N_HEADS, D_MODEL), 1) // HEAD_DIM
    blockmask = hrow == hlane
    q = q_ref[0] * (HEAD_DIM ** -0.5)
    qbd = jnp.concatenate(
        [jnp.where(blockmask, jnp.broadcast_to(q[t:t + 1], (N_HEADS, D_MODEL)), 0.0) for t in range(tp)],
        axis=0).astype(BF16)
    nt = (((1,), (1,)), ((), ()))
    s_c = lax.dot_general(ck_ref[0].astype(BF16), qbd, nt, preferred_element_type=F32)
    s_n = lax.dot_general(kn_ref[0].astype(BF16), qbd, nt, preferred_element_type=F32)
    slope = slope_ref[...]

    def weights(s, key_pos):
        col_t = lax.broadcasted_iota(jnp.int32, s.shape, 1) // N_HEADS
        dist = (w_buf + col_t) - key_pos
        mult = jnp.zeros(s.shape, F32)
        for window, dil in DILATED_CONFIGS:
            hit = (dist >= 0) & (dist <= window) & ((dist & (dil - 1)) == 0)
            mult = mult + jnp.where(hit, 1.0, 0.0)
        sb = jnp.where(mult > 0.0, s - slope * dist.astype(F32), NEG_INF)
        return sb, mult

    sb_c, mult_c = weights(s_c, lax.broadcasted_iota(jnp.int32, s_c.shape, 0))
    sb_n, mult_n = weights(s_n, w_buf + lax.broadcasted_iota(jnp.int32, s_n.shape, 0))
    mx = jnp.maximum(jnp.max(sb_c, axis=0, keepdims=True), jnp.max(sb_n, axis=0, keepdims=True))
    p_c = mult_c * jnp.exp(sb_c - mx)
    p_n = mult_n * jnp.exp(sb_n - mx)
    inv = 1.0 / (jnp.sum(p_c, axis=0, keepdims=True) + jnp.sum(p_n, axis=0, keepdims=True))
    o_all = jnp.dot((p_c * inv).T.astype(BF16), cv_ref[0].astype(BF16), preferred_element_type=F32)
    o_all = o_all + jnp.dot((p_n * inv).T.astype(BF16), vn_ref[0].astype(BF16),
                            preferred_element_type=F32)
    rows = [jnp.sum(jnp.where(blockmask, o_all[t * N_HEADS:(t + 1) * N_HEADS], 0.0), axis=0, keepdims=True)
            for t in range(tp)]
    o_ref[0] = jnp.concatenate(rows, axis=0)
    del t_new


def cache_attention(q, k_new, v_new, cache_k, cache_v, t_new):
    b, tp, d = q.shape
    w_buf = cache_k.shape[1]
    slope_row = jnp.asarray([SLOPES[c % N_HEADS] for c in range(N_HEADS * tp)], F32)[None, :]
    row = lambda i: (i, 0, 0)
    return pl.pallas_call(
        functools.partial(_cache_attn_body, t_new=t_new),
        grid=(b,),
        in_specs=[pl.BlockSpec((1, tp, d), row), pl.BlockSpec((1, tp, d), row), pl.BlockSpec((1, tp, d), row),
                  pl.BlockSpec((1, w_buf, d), row), pl.BlockSpec((1, w_buf, d), row),
                  pl.BlockSpec((1, N_HEADS * tp), lambda i: (0, 0))],
        out_specs=pl.BlockSpec((1, tp, d), row),
        out_shape=jax.ShapeDtypeStruct((b, tp, d), F32),
        compiler_params=_params("parallel"),
        name="cache_attn",
    )(q, k_new, v_new, cache_k, cache_v, slope_row)


def _ssd_body(xbc_ref, z_ref, dt_ref, hist_ref, h0_ref, cw_ref, cb_ref, dtb_ref, alog_ref, dskip_ref, g_ref,
              y_ref, hl_ref, state_ref, tail_ref, *, valid):
    c = pl.program_id(1)

    @pl.when(c == 0)
    def _():
        state_ref[...] = h0_ref[0]
        tail_ref[...] = hist_ref[0]

    xbc = xbc_ref[0]
    ext = jnp.concatenate([tail_ref[...], xbc], axis=0)
    acc = jnp.broadcast_to(cb_ref[...], xbc.shape)
    for w in range(CONV_WIDTH):
        lo = 8 - (CONV_WIDTH - 1) + w
        acc = acc + ext[lo:lo + BLK] * cw_ref[w:w + 1, :]
    tail_ref[...] = xbc[BLK - 8:]
    xc = _silu(acc)
    xs = xc[:, :D_MODEL]
    gn = N_GROUPS * D_STATE

    row = lax.broadcasted_iota(jnp.int32, (BLK, DT_PAD), 0)
    dt = dt_ref[0] + dtb_ref[...]
    dt = jnp.maximum(dt, 0.0) + jnp.log(1.0 + jnp.exp(-jnp.abs(dt)))
    if valid < BLK:
        dt = jnp.where(row < valid, dt, 0.0)
    dta = dt * (-jnp.exp(alog_ref[...]))
    ri = lax.broadcasted_iota(jnp.int32, (BLK, BLK), 0)
    ci = lax.broadcasted_iota(jnp.int32, (BLK, BLK), 1)
    causal = ri >= ci
    tri = jnp.where(causal, 1.0, 0.0)
    acs = jnp.dot(tri, dta, preferred_element_type=F32, precision=lax.Precision.HIGHEST)
    acs_t = acs.T
    dt_t = dt.T
    lane = lax.broadcasted_iota(jnp.int32, (1, BLK), 1)
    lo_half = lane < HEAD_DIM
    nt = (((1,), (1,)), ((), ()))
    hg = N_HEADS // N_GROUPS
    for g in range(N_GROUPS):
        bg = xc[:, D_MODEL + g * D_STATE:D_MODEL + (g + 1) * D_STATE].astype(BF16)
        cg = xc[:, D_MODEL + gn + g * D_STATE:D_MODEL + gn + (g + 1) * D_STATE].astype(BF16)
        cb = lax.dot_general(cg, bg, nt, preferred_element_type=F32)
        for pp in range(hg // 2):
            p = g * (hg // 2) + pp
            sl = slice(p * BLK, (p + 1) * BLK)
            x_pair = xs[:, sl]
            x_pair_b = x_pair.astype(BF16)
            y_pair = None
            dec_in = None
            to_end = None
            dec_out = None
            dsk = None
            for hh in range(2):
                h = 2 * p + hh
                hm = lo_half if hh == 0 else jnp.logical_not(lo_half)
                a_col = acs[:, h:h + 1]
                seg = a_col - acs_t[h:h + 1, :]
                lmat = jnp.exp(jnp.where(causal, seg, NEG_INF))
                mh = (cb * lmat * dt_t[h:h + 1, :]).astype(BF16)
                xm = jnp.where(hm, x_pair_b, jnp.zeros_like(x_pair_b))
                yh = jnp.dot(mh, xm, preferred_element_type=F32)
                y_pair = yh if y_pair is None else y_pair + yh
                a_last = acs[BLK - 1:BLK, h:h + 1]
                e_in = jnp.exp(a_col)
                e_end = jnp.exp(a_last - a_col) * dt[:, h:h + 1]
                e_out = jnp.exp(a_last)
                d_h = dskip_ref[h]
                if hh == 0:
                    dec_in, to_end, dec_out, dsk = e_in, e_end, e_out, d_h
                else:
                    dec_in = jnp.where(lo_half, dec_in, e_in)
                    to_end = jnp.where(lo_half, to_end, e_end)
                    dec_out = jnp.where(lo_half, dec_out, e_out)
                    dsk = jnp.where(lo_half, dsk, d_h)
            h_in = state_ref[sl, :]
            y_off = lax.dot_general(cg, h_in.astype(BF16), nt, preferred_element_type=F32)
            y_pair = y_pair + y_off * dec_in + x_pair * dsk
            xw_t = (x_pair * to_end).T.astype(BF16)
            st = jnp.dot(xw_t, bg, preferred_element_type=F32)
            state_ref[sl, :] = h_in * dec_out.T + st
            y_ref[0, :, sl] = y_pair

    u = y_ref[0] * _silu(z_ref[0])
    gs = D_MODEL // N_GROUPS
    parts = []
    for g in range(N_GROUPS):
        ug = u[:, g * gs:(g + 1) * gs]
        parts.append(ug * lax.rsqrt(jnp.mean(ug * ug, axis=-1, keepdims=True) + RMS_EPS))
    y_ref[0] = jnp.concatenate(parts, axis=1) * g_ref[...]

    @pl.when(c == pl.num_programs(1) - 1)
    def _():
        hl_ref[0] = state_ref[...]


def ssd_mixer(xbc, z, dt_raw, hist, h0, conv_w_t, conv_b, dt_bias, a_log, d_skip, norm_g, valid):
    b, l, _ = xbc.shape
    nc = l // BLK
    hp = N_HEADS * HEAD_DIM
    tok = lambda bi, c: (bi, c, 0)
    seq = lambda bi, c: (bi, 0, 0)
    const = lambda bi, c: (0, 0)
    return pl.pallas_call(
        functools.partial(_ssd_body, valid=valid),
        grid=(b, nc),
        in_specs=[pl.BlockSpec((1, BLK, CONV_DIM), tok), pl.BlockSpec((1, BLK, D_MODEL), tok),
                  pl.BlockSpec((1, BLK, DT_PAD), tok), pl.BlockSpec((1, 8, CONV_DIM), seq),
                  pl.BlockSpec((1, hp, D_STATE), seq),
                  pl.BlockSpec((CONV_WIDTH, CONV_DIM), const), pl.BlockSpec((1, CONV_DIM), const),
                  pl.BlockSpec((1, DT_PAD), const), pl.BlockSpec((1, DT_PAD), const),
                  pl.BlockSpec(memory_space=pltpu.SMEM), pl.BlockSpec((1, D_MODEL), const)],
        out_specs=[pl.BlockSpec((1, BLK, D_MODEL), tok), pl.BlockSpec((1, hp, D_STATE), seq)],
        out_shape=[jax.ShapeDtypeStruct((b, l, D_MODEL), F32), jax.ShapeDtypeStruct((b, hp, D_STATE), F32)],
        scratch_shapes=[pltpu.VMEM((hp, D_STATE), F32), pltpu.VMEM((8, CONV_DIM), F32)],
        compiler_params=_params("parallel", "arbitrary"),
        name="ssd_mixer",
    )(xbc, z, dt_raw, hist, h0, conv_w_t, conv_b, dt_bias, a_log, d_skip, norm_g)


def _outproj_body(*refs, n_cfg):
    o_refs = refs[:n_cfg]
    lse_refs = refs[n_cfg:2 * n_cfg] if n_cfg > 1 else ()
    ssm_ref, h_ref, w_ref, g_ref, b_ref, exp_ref, y_ref = refs[-7:]
    if n_cfg == 1:
        att = o_refs[0][...]
    else:
        lses = [r[...] for r in lse_refs]
        mx = functools.reduce(jnp.maximum, lses)
        es = [jnp.exp(l - mx) for l in lses]
        inv = 1.0 / functools.reduce(lambda a, c: a + c, es)
        att = None
        for o_ref, e in zip(o_refs, es):
            w = e * inv
            hi = w.astype(BF16)
            lo = (w - hi.astype(F32)).astype(BF16)
            wexp = (jnp.dot(hi, exp_ref[...], preferred_element_type=F32)
                    + jnp.dot(lo, exp_ref[...], preferred_element_type=F32))
            term = wexp * o_ref[...]
            att = term if att is None else att + term
    out = jnp.dot(att.astype(BF16), w_ref[:D_MODEL, :], preferred_element_type=F32)
    out = out + jnp.dot(ssm_ref[...].astype(BF16), w_ref[D_MODEL:, :], preferred_element_type=F32)
    y_ref[...] = _layer_norm(ALPHA * h_ref[...] + out, g_ref[...], b_ref[...])


def out_proj(att_parts, lse_parts, ssm, h, w_out_b, ln_g, ln_b, head_expand):
    m = h.shape[0]
    tm = min(256, m)
    n_cfg = len(att_parts)
    tok = lambda i: (i, 0)
    const = lambda i: (0, 0)
    ins = list(att_parts) + (list(lse_parts) if n_cfg > 1 else [])
    specs = [pl.BlockSpec((tm, D_MODEL), tok)] * n_cfg + ([pl.BlockSpec((tm, BLK), tok)] * n_cfg if n_cfg > 1 else [])
    return pl.pallas_call(
        functools.partial(_outproj_body, n_cfg=n_cfg),
        grid=(m // tm,),
        in_specs=specs + [pl.BlockSpec((tm, D_MODEL), tok), pl.BlockSpec((tm, D_MODEL), tok),
                          pl.BlockSpec((2 * D_MODEL, D_MODEL), const), pl.BlockSpec((1, D_MODEL), const),
                          pl.BlockSpec((1, D_MODEL), const), pl.BlockSpec((BLK, D_MODEL), const)],
        out_specs=pl.BlockSpec((tm, D_MODEL), tok),
        out_shape=jax.ShapeDtypeStruct((m, D_MODEL), F32),
        compiler_params=_params("parallel"),
        name=f"out_proj_c{n_cfg}",
    )(*ins, ssm, h, w_out_b, ln_g, ln_b, head_expand)


def _fold_body(sk_ref, wq_ref, o_ref):
    o_ref[0] = lax.dot_general(sk_ref[0], wq_ref[...], (((1,), (1,)), ((), ())),
                               preferred_element_type=F32, precision=lax.Precision.HIGHEST)


def fold_scores(peer_wq, peer_keys):
    dk = peer_keys.shape[-1]
    nblk = PEER_HEADS * 2
    folded = pl.pallas_call(
        _fold_body,
        grid=(nblk,),
        in_specs=[pl.BlockSpec((1, PEER_NKEYS, dk), lambda i: (i % 2, 0, 0)),
                  pl.BlockSpec((D_MODEL, dk), lambda i: (0, i))],
        out_specs=pl.BlockSpec((1, PEER_NKEYS, D_MODEL), lambda i: (i, 0, 0)),
        out_shape=jax.ShapeDtypeStruct((nblk, PEER_NKEYS, D_MODEL), F32),
        compiler_params=_params("parallel"),
        name="peer_fold",
    )(peer_keys, peer_wq)
    folded = folded.reshape(PEER_HEADS, 2, PEER_NKEYS, D_MODEL).transpose(1, 2, 0, 3)
    return folded.reshape(2 * PEER_NKEYS * PEER_HEADS, D_MODEL)


_CANDS = tuple((k, l) for k in range(PEER_TOPK) for l in range(PEER_TOPK) if (k + 1) * (l + 1) <= PEER_TOPK)


def _topk_extract(score_ref, work_ref, rank_ref, vals_ref):
    shape = work_ref.shape
    eio = lax.broadcasted_iota(jnp.int32, shape, 0)
    work_ref[...] = score_ref[...]
    rank_ref[...] = jnp.full(shape, 99.0, F32)

    def body(t, carry):
        sw = work_ref[...]
        m = jnp.max(sw, axis=0)
        idx = jnp.min(jnp.where(sw == m[None], eio, PEER_NKEYS), axis=0)
        sel = eio == idx[None]
        work_ref[...] = jnp.where(sel, NEG_INF, sw)
        rank_ref[...] = jnp.where(sel, t.astype(F32), rank_ref[...])
        vals_ref[t] = m
        return carry

    lax.fori_loop(0, PEER_TOPK, body, 0)


def _cmp_exchange(x, i, j):
    a, b = x[i], x[j]
    if b is None:
        return
    if a is None:
        x[i], x[j] = b, None
        return
    x[i], x[j] = jnp.maximum(a, b), jnp.minimum(a, b)


def _sort_desc(x):
    x = list(x)
    n = len(x)
    k = 2
    while k <= n:
        j = k // 2
        while j > 0:
            for i in range(n):
                l = i ^ j
                if l > i:
                    if (i & k) == 0:
                        _cmp_exchange(x, i, l)
                    else:
                        _cmp_exchange(x, l, i)
            j //= 2
        k *= 2
    return x


def _merge_top(a, b):
    n = len(a)
    m = []
    for i in range(n):
        u, v = a[i], b[n - 1 - i]
        m.append(v if u is None else (u if v is None else jnp.maximum(u, v)))
    j = n // 2
    while j > 0:
        for i in range(n):
            if (i & j) == 0:
                _cmp_exchange(m, i, i + j)
        j //= 2
    return m


def _top_sorted(tiles):
    groups = [_sort_desc(tiles[g:g + PEER_TOPK]) for g in range(0, len(tiles), PEER_TOPK)]
    while len(groups) > 1:
        groups = [_merge_top(groups[g], groups[g + 1]) for g in range(0, len(groups), 2)]
    return groups[0]


def _store_head_major(tmp_scr, val, out_ref):
    nk, nh = PEER_NKEYS, PEER_HEADS
    tmp_scr[...] = val.reshape(nk * nh, val.shape[-1])
    for h in range(nh):
        out_ref[h] = tmp_scr[pl.ds(h, nk, stride=nh), :].astype(BF16)


def _route_by_value(s_scr, cmap_ref, e1_ref, r2_ref, w2_ref, tmp_scr):
    topk = PEER_TOPK
    a = _top_sorted([s_scr[0, e] for e in range(PEER_NKEYS)])
    b = _top_sorted([s_scr[1, e] for e in range(PEER_NKEYS)])
    row_len = [topk // (k + 1) for k in range(topk)]
    sums = [[a[k] + b[l] for l in range(row_len[k])] for k in range(topk)]
    pad = lambda lst: list(lst) + [None] * (topk - len(lst))
    top = sums[0]
    lists = [sums[1], [sums[k][0] for k in range(topk // 2, topk)]] + [sums[k] for k in range(2, topk // 2)]
    for lst in lists:
        top = _merge_top(top, pad(lst))
    tau = top[topk - 1]
    z = None
    for t in range(topk):
        ez = jnp.exp(top[t] - top[0])
        z = ez if z is None else z + ez
    inv_z = 1.0 / z
    counts = []
    for k in range(topk):
        ck = None
        for l in range(row_len[k]):
            one = jnp.where(sums[k][l] >= tau, 1.0, 0.0)
            ck = one if ck is None else ck + one
        counts.append(ck)
    total = functools.reduce(lambda u, v: u + v, counts)
    floor1 = a[0]
    for k in range(1, topk):
        floor1 = jnp.where(counts[k] >= 1.0, a[k], floor1)

    s1 = s_scr[0]
    n1 = jnp.sum(jnp.where(s1 >= a[topk - 1][None], 1.0, 0.0), axis=0)
    cmap = jnp.where(s1 >= floor1[None], 1.0, 0.0)
    for k in range(topk // 2):
        cmap = jnp.where(s1 == a[k][None], counts[k][None], cmap)
    cmap_ref[...] = cmap
    e1_ref[...] = jnp.exp(s1 - a[0][None])
    s2 = s_scr[1]
    n2 = jnp.sum(jnp.where(s2 >= b[topk - 1][None], 1.0, 0.0), axis=0)
    rank2 = jnp.full(s2.shape, 99.0, F32)
    for l in range(topk):
        rank2 = jnp.where(s2 == b[l][None], float(l), rank2)
    _store_head_major(tmp_scr, rank2, r2_ref)
    _store_head_major(tmp_scr, jnp.exp(s2 - b[0][None]) * inv_z[None], w2_ref)
    ftop = float(topk)
    return ((total != ftop) | ((n1 > ftop) & (counts[topk - 1] >= 1.0)) | ((n2 > ftop) & (counts[0] >= ftop)))


def _route_body(h_ref, ws_ref, cmap_ref, e1_ref, r2_ref, w2_ref,
                s_scr, work_scr, rank1_scr, rank2_scr, a_scr, b_scr, tmp_scr):
    tl = s_scr.shape[-1]
    nk, nh = PEER_NKEYS, PEER_HEADS
    s_t = lax.dot_general(ws_ref[...], h_ref[...].astype(BF16), (((1,), (1,)), ((), ())),
                          preferred_element_type=F32)
    s_scr[...] = s_t.reshape(2, nk, nh, tl)
    unsure = _route_by_value(s_scr, cmap_ref, e1_ref, r2_ref, w2_ref, tmp_scr)

    @pl.when(jnp.max(jnp.where(unsure, 1.0, 0.0)) > 0.0)
    def _():
        _route_by_index(s_scr, cmap_ref, e1_ref, r2_ref, w2_ref,
                        work_scr, rank1_scr, rank2_scr, a_scr, b_scr, tmp_scr)


def _route_by_index(s_scr, cmap_ref, e1_ref, r2_ref, w2_ref,
                    work_scr, rank1_scr, rank2_scr, a_scr, b_scr, tmp_scr):
    _topk_extract(s_scr.at[0], work_scr, rank1_scr, a_scr)
    _topk_extract(s_scr.at[1], work_scr, rank2_scr, b_scr)

    cvals = [a_scr[k] + b_scr[l] for k, l in _CANDS]
    nc = len(_CANDS)
    after = [None] * nc
    before = [None] * nc
    for x in range(nc):
        for y in range(x + 1, nc):
            gt = jnp.where(cvals[y] > cvals[x], 1.0, 0.0)
            after[x] = gt if after[x] is None else after[x] + gt
            before[y] = gt if before[y] is None else before[y] + gt
    z = None
    counts = [None] * PEER_TOPK
    for x, (k, l) in enumerate(_CANDS):
        ahead = float(x)
        if after[x] is not None:
            ahead = ahead + after[x]
        if before[x] is not None:
            ahead = ahead - before[x]
        chosen = ahead < float(PEER_TOPK)
        one = jnp.where(chosen, 1.0, 0.0)
        counts[k] = one if counts[k] is None else counts[k] + one
        ez = jnp.where(chosen, jnp.exp(cvals[x] - cvals[0]), 0.0)
        z = ez if z is None else z + ez
    inv_z = 1.0 / z

    rank1 = rank1_scr[...]
    cmap = jnp.zeros(rank1.shape, F32)
    for k in range(PEER_TOPK):
        cmap = jnp.where(rank1 == float(k), counts[k][None], cmap)
    cmap_ref[...] = cmap
    e1_ref[...] = jnp.where(rank1 < float(PEER_TOPK), jnp.exp(s_scr[0] - a_scr[0][None]), 0.0)
    rank2 = rank2_scr[...]
    w2 = jnp.where(rank2 < float(PEER_TOPK), jnp.exp(s_scr[1] - b_scr[0][None]) * inv_z[None], 0.0)
    _store_head_major(tmp_scr, rank2, r2_ref)
    _store_head_major(tmp_scr, w2, w2_ref)


def peer_route(h2d, ws_t):
    m = h2d.shape[0]
    tl = 128
    nk, nh = PEER_NKEYS, PEER_HEADS
    tok3 = lambda i: (0, 0, i)
    return pl.pallas_call(
        _route_body,
        grid=(m // tl,),
        in_specs=[pl.BlockSpec((tl, D_MODEL), lambda i: (i, 0)),
                  pl.BlockSpec((2 * nk * nh, D_MODEL), lambda i: (0, 0))],
        out_specs=[pl.BlockSpec((nk, nh, tl), tok3), pl.BlockSpec((nk, nh, tl), tok3),
                   pl.BlockSpec((nh, nk, tl), tok3), pl.BlockSpec((nh, nk, tl), tok3)],
        out_shape=[jax.ShapeDtypeStruct((nk, nh, m), F32), jax.ShapeDtypeStruct((nk, nh, m), F32),
                   jax.ShapeDtypeStruct((nh, nk, m), BF16), jax.ShapeDtypeStruct((nh, nk, m), BF16)],
        scratch_shapes=[pltpu.VMEM((2, nk, nh, tl), F32), pltpu.VMEM((nk, nh, tl), F32),
                        pltpu.VMEM((nk, nh, tl), F32), pltpu.VMEM((nk, nh, tl), F32),
                        pltpu.VMEM((PEER_TOPK, nh, tl), F32), pltpu.VMEM((PEER_TOPK, nh, tl), F32),
                        pltpu.VMEM((nk * nh, tl), F32)],
        compiler_params=_params("parallel"),
        name="peer_route",
    )(h2d, ws_t)


def _experts_body(h_ref, u_ref, vt_ref, cmap_ref, e1_ref, r2_ref, w2_ref, g_ref, b_ref, y_ref,
                  acc_ref, hid_ref, *, rows_per_step):
    j = pl.program_id(1)
    tm = h_ref.shape[0]

    @pl.when(j == 0)
    def _():
        acc_ref[...] = jnp.zeros_like(acc_ref)

    hb = h_ref[...].astype(BF16)
    a_t = lax.dot_general(u_ref[...], hb, (((1,), (1,)), ((), ())), preferred_element_type=F32)
    for ib in range(rows_per_step):
        i = j * rows_per_step + ib
        sub = 16
        grp = (PEER_NKEYS // sub, sub, tm)
        gate = jnp.zeros(grp, BF16)
        for h in range(PEER_HEADS):
            cm = jnp.broadcast_to(cmap_ref[i, h:h + 1, :], (sub, tm)).astype(BF16)
            e1 = jnp.broadcast_to(e1_ref[i, h:h + 1, :], (sub, tm)).astype(BF16)
            w2 = w2_ref[h].reshape(grp)
            gate = gate + jnp.where(r2_ref[h].reshape(grp) < cm[None], w2, jnp.zeros_like(w2)) * e1[None]
        a = a_t[ib * PEER_NKEYS:(ib + 1) * PEER_NKEYS]
        act = 0.5 * a * (1.0 + lax.erf(a * (2.0 ** -0.5)))
        hid_ref[ib * PEER_NKEYS:(ib + 1) * PEER_NKEYS, :] = gate.reshape(PEER_NKEYS, tm) * act.astype(BF16)
    acc_ref[...] += jnp.dot(vt_ref[...], hid_ref[...], preferred_element_type=F32)

    @pl.when(j == pl.num_programs(1) - 1)
    def _():
        y_ref[...] = _layer_norm(ALPHA * h_ref[...] + acc_ref[...].T, g_ref[...], b_ref[...])


def peer_experts(h2d, u_b, vt_b, cmap, e1map, rank2, w2, ln_g, ln_b):
    m = h2d.shape[0]
    tm = min(512, m)
    te = 1024
    n_exp = u_b.shape[0]
    rows = te // PEER_NKEYS
    nk, nh = PEER_NKEYS, PEER_HEADS
    tok = lambda i, j: (i, 0)
    tok3 = lambda i, j: (0, 0, i)
    const = lambda i, j: (0, 0)
    return pl.pallas_call(
        functools.partial(_experts_body, rows_per_step=rows),
        grid=(m // tm, n_exp // te),
        in_specs=[pl.BlockSpec((tm, D_MODEL), tok),
                  pl.BlockSpec((te, D_MODEL), lambda i, j: (j, 0)),
                  pl.BlockSpec((D_MODEL, te), lambda i, j: (0, j)),
                  pl.BlockSpec((nk, nh, tm), tok3), pl.BlockSpec((nk, nh, tm), tok3),
                  pl.BlockSpec((nh, nk, tm), tok3), pl.BlockSpec((nh, nk, tm), tok3),
                  pl.BlockSpec((1, D_MODEL), const), pl.BlockSpec((1, D_MODEL), const)],
        out_specs=pl.BlockSpec((tm, D_MODEL), tok),
        out_shape=jax.ShapeDtypeStruct((m, D_MODEL), F32),
        scratch_shapes=[pltpu.VMEM((D_MODEL, tm), F32), pltpu.VMEM((te, tm), BF16)],
        compiler_params=_params("parallel", "arbitrary"),
        name="peer_experts",
    )(h2d, u_b, vt_b, cmap, e1map, rank2, w2, ln_g, ln_b)


def _transpose_body(x_ref, o_ref):
    o_ref[...] = x_ref[...].T.astype(BF16)


def transpose_cast(x):
    n, d = x.shape
    tn = 512
    return pl.pallas_call(
        _transpose_body,
        grid=(n // tn,),
        in_specs=[pl.BlockSpec((tn, d), lambda i: (i, 0))],
        out_specs=pl.BlockSpec((d, tn), lambda i: (0, i)),
        out_shape=jax.ShapeDtypeStruct((d, n), BF16),
        compiler_params=_params("parallel"),
        name="transpose_cast",
    )(x)


def peer_layer(h2d, ws_t, u_b, vt_b, ln_g, ln_b):
    cmap, e1map, rank2, w2 = peer_route(h2d, ws_t)
    return peer_experts(h2d, u_b, vt_b, cmap, e1map, rank2, w2, ln_g, ln_b)


def kernel(x_prompt, x_sample, cache_k, cache_v, state_conv, state_ssm, w_in, conv_w, conv_b, dt_bias, a_log,
           d_skip, ssm_norm_g, w_out, ln1_g, ln1_b, peer_wq, peer_keys, peer_u, peer_v, ln2_g, ln2_b):
    bp, sp, d = x_prompt.shape
    bs, ts, _ = x_sample.shape
    w_buf = cache_k.shape[2]
    hp = N_HEADS * HEAD_DIM
    tq = 8
    hpv = x_prompt.reshape(bp * sp, d)
    hsv = x_sample.reshape(bs * ts, d)
    lane_head = jnp.arange(D_MODEL)[None, :] // HEAD_DIM
    head_expand = (jnp.arange(BLK)[:, None] == lane_head).astype(BF16)
    pad_lanes = lambda a: jnp.pad(a[None, :], ((0, 0), (0, DT_PAD - a.shape[0])))
    outs = {n: [] for n in ("pk", "pv", "pc", "ps", "sk", "sv", "sc", "ss")}

    for l in range(DEPTH):
        w_pad = jnp.pad(w_in[l], ((0, 0), (0, DT_PAD - N_HEADS))).astype(BF16)
        w_out_b = w_out[l].astype(BF16)
        cw_t = conv_w[l].T
        cb = conv_b[l][None, :]
        dtb, alog = pad_lanes(dt_bias[l]), pad_lanes(a_log[l])
        ng = ssm_norm_g[l][None, :]
        g1, b1, g2, b2 = ln1_g[l][None, :], ln1_b[l][None, :], ln2_g[l][None, :], ln2_b[l][None, :]
        ws_t = fold_scores(peer_wq[l], peer_keys[l]).astype(BF16)
        u_b = peer_u[l].astype(BF16)
        vt_b = transpose_cast(peer_v[l])

        q, k, v, z, xbc, dtr = in_proj(hpv, w_pad)
        q3, k3, v3 = (a.reshape(bp, sp, d) for a in (q, k, v))
        att = [band_attention(q3, k3, v3, dil) for _, dil in DILATED_CONFIGS]
        y, h_last = ssd_mixer(xbc.reshape(bp, sp, CONV_DIM), z.reshape(bp, sp, d), dtr.reshape(bp, sp, DT_PAD),
                              jnp.zeros((bp, 8, CONV_DIM), F32), jnp.zeros((bp, hp, D_STATE), F32),
                              cw_t, cb, dtb, alog, d_skip[l], ng, valid=BLK)
        hpv = out_proj([o.reshape(bp * sp, d) for o, _ in att], [s.reshape(bp * sp, BLK) for _, s in att],
                       y.reshape(bp * sp, d), hpv, w_out_b, g1, b1, head_expand)
        hpv = peer_layer(hpv, ws_t, u_b, vt_b, g2, b2)
        keep = max(0, sp - MAX_WINDOW)
        outs["pk"].append(k3[:, keep:].reshape(bp, sp - keep, N_HEADS, HEAD_DIM))
        outs["pv"].append(v3[:, keep:].reshape(bp, sp - keep, N_HEADS, HEAD_DIM))
        outs["pc"].append(xbc.reshape(bp, sp, CONV_DIM)[:, sp - (CONV_WIDTH - 1):])
        outs["ps"].append(h_last.reshape(bp, N_HEADS, HEAD_DIM, D_STATE))

        q, k, v, z, xbc, dtr = in_proj(hsv, w_pad)
        pad_t = lambda a, n: jnp.pad(a.reshape(bs, ts, a.shape[-1]), ((0, 0), (0, n - ts), (0, 0)))
        ck = cache_k[l].reshape(bs, w_buf, d)
        cv = cache_v[l].reshape(bs, w_buf, d)
        att_s = cache_attention(pad_t(q, tq), pad_t(k, tq), pad_t(v, tq), ck, cv, ts)[:, :ts].reshape(bs * ts, d)
        hist = jnp.pad(state_conv[l], ((0, 0), (8 - (CONV_WIDTH - 1), 0), (0, 0)))
        y, h_last = ssd_mixer(pad_t(xbc, BLK), pad_t(z, BLK), pad_t(dtr, BLK), hist,
                              state_ssm[l].reshape(bs, hp, D_STATE), cw_t, cb, dtb, alog, d_skip[l], ng, valid=ts)
        y = y[:, :ts].reshape(bs * ts, d)
        hsv = out_proj([att_s], [], y, hsv, w_out_b, g1, b1, head_expand)
        hsv = peer_layer(hsv, ws_t, u_b, vt_b, g2, b2)
        k_all = jnp.concatenate([ck, k.reshape(bs, ts, d)], axis=1)
        v_all = jnp.concatenate([cv, v.reshape(bs, ts, d)], axis=1)
        keep = max(0, w_buf + ts - MAX_WINDOW)
        outs["sk"].append(k_all[:, keep:].reshape(bs, -1, N_HEADS, HEAD_DIM))
        outs["sv"].append(v_all[:, keep:].reshape(bs, -1, N_HEADS, HEAD_DIM))
        conv_ext = jnp.concatenate([state_conv[l], xbc.reshape(bs, ts, CONV_DIM)], axis=1)
        outs["sc"].append(conv_ext[:, -(CONV_WIDTH - 1):])
        outs["ss"].append(h_last.reshape(bs, N_HEADS, HEAD_DIM, D_STATE))

    st = {n: jnp.stack(v) for n, v in outs.items()}
    return (hpv.reshape(bp, sp, d), hsv.reshape(bs, ts, d), st["pk"], st["pv"], st["pc"], st["ps"],
            st["sk"], st["sv"], st["sc"], st["ss"])
```

```python
import functools
import math

import jax
import jax.numpy as jnp
from jax import lax
from jax.experimental import pallas as pl
from jax.experimental.pallas import tpu as pltpu

F32 = jnp.float32
BF16 = jnp.bfloat16

D_MODEL = 1024
DEPTH = 2
HEAD_DIM = 64
N_HEADS = 16
N_GROUPS = 2
D_STATE = 128
CONV_WIDTH = 4
CONV_DIM = D_MODEL + 2 * N_GROUPS * D_STATE
DILATED_CONFIGS = ((128, 1), (512, 4), (2048, 16))
MAX_WINDOW = 2048
BLK = 128
PEER_HEADS = 8
PEER_NKEYS = 128
PEER_TOPK = 16
LN_EPS = 1e-5
RMS_EPS = 1e-5
ALPHA = (2 * DEPTH) ** 0.25
DT_PAD = 128
PROJ_WIDTHS = (D_MODEL, D_MODEL, D_MODEL, D_MODEL, CONV_DIM, DT_PAD)
VMEM_LIMIT = 56 * 1024 * 1024
SLOPES = tuple(2.0 ** (-8.0 * (h + 1) / N_HEADS) for h in range(N_HEADS))
NEG_INF = float("-inf")


def _params(*sem, flags=None):
    return pltpu.CompilerParams(dimension_semantics=sem, vmem_limit_bytes=VMEM_LIMIT, flags=flags)


def _layer_norm(x, g, b):
    mu = jnp.mean(x, axis=-1, keepdims=True)
    xc = x - mu
    var = jnp.mean(xc * xc, axis=-1, keepdims=True)
    return xc * lax.rsqrt(var + LN_EPS) * g + b


def _silu(x):
    return x * (1.0 / (1.0 + jnp.exp(-x)))


def _inproj_body(x_ref, w_ref, *out_refs):
    x = x_ref[...].astype(BF16)
    off = 0
    for ref in out_refs:
        n = ref.shape[-1]
        ref[...] = jnp.dot(x, w_ref[:, off:off + n], preferred_element_type=F32)
        off += n


def in_proj(x2d, w_pad):
    m = x2d.shape[0]
    tm = min(256, m)
    n_all = sum(PROJ_WIDTHS)
    return pl.pallas_call(
        _inproj_body,
        grid=(m // tm,),
        in_specs=[pl.BlockSpec((tm, D_MODEL), lambda i: (i, 0)),
                  pl.BlockSpec((D_MODEL, n_all), lambda i: (0, 0))],
        out_specs=[pl.BlockSpec((tm, n), lambda i: (i, 0)) for n in PROJ_WIDTHS],
        out_shape=[jax.ShapeDtypeStruct((m, n), F32) for n in PROJ_WIDTHS],
        compiler_params=_params("parallel"),
        name="in_proj",
    )(x2d, w_pad)


def _band_attn_body(q_ref, kp_ref, kc_ref, vp_ref, vc_ref, o_ref, lse_ref, *, dil):
    n = pl.program_id(2)
    q = (q_ref[0] * (HEAD_DIM ** -0.5)).astype(BF16)
    k2 = jnp.concatenate([kp_ref[0], kc_ref[0]], axis=0).astype(BF16)
    v2 = jnp.concatenate([vp_ref[0], vc_ref[0]], axis=0).astype(BF16)
    qi = lax.broadcasted_iota(jnp.int32, (BLK, 2 * BLK), 0)
    kj = lax.broadcasted_iota(jnp.int32, (BLK, 2 * BLK), 1)
    back = BLK + qi - kj
    valid = (back >= 0) & (back <= BLK) & ((kj >= BLK) | (n > 0))
    negdist = -(back * dil).astype(F32)
    lane = lax.broadcasted_iota(jnp.int32, (1, BLK), 1)
    lse_all = jnp.zeros((BLK, BLK), F32)
    for p in range(N_HEADS // 2):
        sl = slice(p * BLK, (p + 1) * BLK)
        qp, kp, vp = q[:, sl], k2[:, sl], v2[:, sl]
        acc = None
        for hh in range(2):
            h = 2 * p + hh
            hm = (lane < HEAD_DIM) if hh == 0 else (lane >= HEAD_DIM)
            qm = jnp.where(hm, qp, jnp.zeros_like(qp))
            s = lax.dot_general(qm, kp, (((1,), (1,)), ((), ())), preferred_element_type=F32)
            s = jnp.where(valid, s + SLOPES[h] * negdist, NEG_INF)
            mx = jnp.max(s, axis=1, keepdims=True)
            e = jnp.exp(s - mx)
            l = jnp.sum(e, axis=1, keepdims=True)
            vm = jnp.where(hm, vp, jnp.zeros_like(vp))
            pv = jnp.dot((e * (1.0 / l)).astype(BF16), vm, preferred_element_type=F32)
            acc = pv if acc is None else acc + pv
            lse_all = jnp.where(lane == h, mx + jnp.log(l), lse_all)
        o_ref[0, :, sl] = acc
    lse_ref[0] = lse_all


def _band_attn_strided_body(slope_ref, q_ref, kp_ref, kc_ref, vp_ref, vc_ref, o_ref, lse_ref, *, dil):
    n = pl.program_id(1)
    p = pl.program_id(2)

    @pl.when(p == 0)
    def _():
        lse_ref[...] = jnp.zeros_like(lse_ref)

    qi = lax.broadcasted_iota(jnp.int32, (BLK, 2 * BLK), 0)
    kj = lax.broadcasted_iota(jnp.int32, (BLK, 2 * BLK), 1)
    back = BLK + qi - kj
    valid = (back >= 0) & (back <= BLK) & ((kj >= BLK) | (n > 0))
    negdist = -(back * dil).astype(F32)
    lane = lax.broadcasted_iota(jnp.int32, (1, BLK), 1)
    for r in range(dil):
        rows = pl.ds(r, BLK, stride=dil)
        qp = (q_ref[0, rows, :] * (HEAD_DIM ** -0.5)).astype(BF16)
        kp = jnp.concatenate([kp_ref[0, rows, :], kc_ref[0, rows, :]], axis=0).astype(BF16)
        vp = jnp.concatenate([vp_ref[0, rows, :], vc_ref[0, rows, :]], axis=0).astype(BF16)
        lse_rows = lse_ref[0, rows, :]
        acc = None
        for hh in range(2):
            h = 2 * p + hh
            hm = (lane < HEAD_DIM) if hh == 0 else (lane >= HEAD_DIM)
            qm = jnp.where(hm, qp, jnp.zeros_like(qp))
            s = lax.dot_general(qm, kp, (((1,), (1,)), ((), ())), preferred_element_type=F32)
            s = jnp.where(valid, s + slope_ref[h] * negdist, NEG_INF)
            mx = jnp.max(s, axis=1, keepdims=True)
            e = jnp.exp(s - mx)
            l = jnp.sum(e, axis=1, keepdims=True)
            vm = jnp.where(hm, vp, jnp.zeros_like(vp))
            pv = jnp.dot((e * (1.0 / l)).astype(BF16), vm, preferred_element_type=F32)
            acc = pv if acc is None else acc + pv
            lse_rows = jnp.where(lane == h, mx + jnp.log(l), lse_rows)
        o_ref[0, rows, :] = acc
        lse_ref[0, rows, :] = lse_rows


def band_attention(q, k, v, dil):
    b, s, d = q.shape
    if dil == 1:
        nblk = s // BLK
        cur = lambda bi, r, n: (bi, n, r)
        prev = lambda bi, r, n: (bi, jnp.maximum(n - 1, 0), r)
        return pl.pallas_call(
            functools.partial(_band_attn_body, dil=dil),
            grid=(b, 1, nblk),
            in_specs=[pl.BlockSpec((1, BLK, d), cur),
                      pl.BlockSpec((1, BLK, d), prev), pl.BlockSpec((1, BLK, d), cur),
                      pl.BlockSpec((1, BLK, d), prev), pl.BlockSpec((1, BLK, d), cur)],
            out_specs=[pl.BlockSpec((1, BLK, d), cur), pl.BlockSpec((1, BLK, BLK), cur)],
            out_shape=[jax.ShapeDtypeStruct((b, s, d), F32), jax.ShapeDtypeStruct((b, s, BLK), F32)],
            compiler_params=_params("parallel", "parallel", "arbitrary"),
            name="band_attn_d1",
        )(q, k, k, v, v)
    unit = BLK * dil
    cur = lambda bi, n, p: (bi, n, p)
    prev = lambda bi, n, p: (bi, jnp.maximum(n - 1, 0), p)
    return pl.pallas_call(
        functools.partial(_band_attn_strided_body, dil=dil),
        grid=(b, s // unit, d // BLK),
        in_specs=[pl.BlockSpec(memory_space=pltpu.SMEM),
                  pl.BlockSpec((1, unit, BLK), cur),
                  pl.BlockSpec((1, unit, BLK), prev), pl.BlockSpec((1, unit, BLK), cur),
                  pl.BlockSpec((1, unit, BLK), prev), pl.BlockSpec((1, unit, BLK), cur)],
        out_specs=[pl.BlockSpec((1, unit, BLK), cur), pl.BlockSpec((1, unit, BLK), lambda bi, n, p: (bi, n, 0))],
        out_shape=[jax.ShapeDtypeStruct((b, s, d), F32), jax.ShapeDtypeStruct((b, s, BLK), F32)],
        compiler_params=_params("parallel", "parallel", "arbitrary"),
        name=f"band_attn_d{dil}",
    )(jnp.asarray(SLOPES, F32), q, k, k, v, v)


HEADS_PER_STEP = 4


def _cache_attn_body(slope_ref, q_ref, kn_ref, vn_ref, ck_ref, cv_ref, *rest, t_new):
    o_ref, nk_ref, nv_ref = rest[-3:]
    c = pl.program_id(1)
    fc, w_buf = ck_ref.shape[2], ck_ref.shape[3]
    tp = q_ref.shape[1]
    hc = fc // HEAD_DIM
    nrow = tp * hc
    hrow = lax.broadcasted_iota(jnp.int32, (hc, fc), 0)
    hlane = lax.broadcasted_iota(jnp.int32, (hc, fc), 1) // HEAD_DIM
    blockmask = hrow == hlane
    q = q_ref[0] * (HEAD_DIM ** -0.5)
    qbd = jnp.concatenate(
        [jnp.where(blockmask, jnp.broadcast_to(q[t:t + 1], (hc, fc)), 0.0) for t in range(tp)],
        axis=0).astype(BF16)
    k_t = ck_ref[0, 0]
    v_t = cv_ref[0, 0]
    kn = kn_ref[0]
    vn = vn_ref[0]
    nt = (((1,), (1,)), ((), ()))
    s_c = jnp.dot(qbd, k_t.astype(BF16), preferred_element_type=F32)
    s_n = lax.dot_general(qbd, kn.astype(BF16), nt, preferred_element_type=F32)
    rix = lax.broadcasted_iota(jnp.int32, (nrow, 1), 0)
    row_t = rix // hc
    slope = jnp.zeros((nrow, 1), F32)
    for hh in range(hc):
        slope = jnp.where(rix % hc == hh, slope_ref[c * hc + hh], slope)

    def weights(s, key_pos):
        dist = (w_buf + row_t) - key_pos
        mult = jnp.zeros(s.shape, F32)
        for window, dil in DILATED_CONFIGS:
            hit = (dist >= 0) & (dist <= window) & ((dist & (dil - 1)) == 0)
            mult = mult + jnp.where(hit, 1.0, 0.0)
        sb = jnp.where(mult > 0.0, s - slope * dist.astype(F32), NEG_INF)
        return sb, mult

    sb_c, mult_c = weights(s_c, lax.broadcasted_iota(jnp.int32, s_c.shape, 1))
    sb_n, mult_n = weights(s_n, w_buf + lax.broadcasted_iota(jnp.int32, s_n.shape, 1))
    mx = jnp.maximum(jnp.max(sb_c, axis=1, keepdims=True), jnp.max(sb_n, axis=1, keepdims=True))
    p_c = mult_c * jnp.exp(sb_c - mx)
    p_n = mult_n * jnp.exp(sb_n - mx)
    inv = 1.0 / (jnp.sum(p_c, axis=1, keepdims=True) + jnp.sum(p_n, axis=1, keepdims=True))
    o_all = lax.dot_general((p_c * inv).astype(BF16), v_t.astype(BF16), nt, preferred_element_type=F32)
    o_all = o_all + jnp.dot((p_n * inv).astype(BF16), vn.astype(BF16), preferred_element_type=F32)
    rows = [jnp.sum(jnp.where(blockmask, o_all[t * hc:(t + 1) * hc], 0.0), axis=0, keepdims=True)
            for t in range(tp)]
    o_ref[0] = jnp.concatenate(rows, axis=0)

    lane = lax.broadcasted_iota(jnp.int32, (1, BLK), 1)

    def shifted(old_t, new_rows, out_ref):
        rolled = pltpu.roll(old_t, w_buf - t_new, 1)
        new_t = new_rows.T
        last = rolled[:, w_buf - BLK:]
        for t in range(t_new):
            last = jnp.where(lane == BLK - t_new + t, new_t[:, t:t + 1], last)
        out_ref[0, 0, :, :w_buf - BLK] = rolled[:, :w_buf - BLK]
        out_ref[0, 0, :, w_buf - BLK:] = last

    shifted(k_t, kn, nk_ref)
    shifted(v_t, vn, nv_ref)


def cache_attention(q, k_new, v_new, cache_kt, cache_vt, layer, prev, t_new):
    b, tp, d = q.shape
    w_buf = cache_kt.shape[-1]
    fc = HEADS_PER_STEP * HEAD_DIM
    tok = lambda i, c: (i, 0, c)
    win = lambda i, c: (layer, i, c, 0)
    ins = [jnp.asarray(SLOPES, F32), q, k_new, v_new, cache_kt, cache_vt]
    specs = [pl.BlockSpec(memory_space=pltpu.SMEM),
             pl.BlockSpec((1, tp, fc), tok), pl.BlockSpec((1, tp, fc), tok), pl.BlockSpec((1, tp, fc), tok),
             pl.BlockSpec((1, 1, fc, w_buf), win), pl.BlockSpec((1, 1, fc, w_buf), win)]
    aliases = {}
    if prev is not None:
        aliases = {len(ins): 1, len(ins) + 1: 2}
        ins += list(prev)
        specs += [pl.BlockSpec(memory_space=pl.ANY)] * 2
    win_shape = jax.ShapeDtypeStruct(cache_kt.shape, F32)
    return pl.pallas_call(
        functools.partial(_cache_attn_body, t_new=t_new),
        grid=(b, d // fc),
        in_specs=specs,
        out_specs=[pl.BlockSpec((1, tp, fc), tok), pl.BlockSpec((1, 1, fc, w_buf), win),
                   pl.BlockSpec((1, 1, fc, w_buf), win)],
        out_shape=[jax.ShapeDtypeStruct((b, tp, d), F32), win_shape, win_shape],
        input_output_aliases=aliases,
        compiler_params=_params("parallel", "parallel"),
        name="cache_attn",
    )(*ins)


def _ssd_body(xbc_ref, z_ref, dt_ref, hist_ref, h0_ref, cw_ref, cb_ref, dtb_ref, alog_ref, dskip_ref, g_ref,
              y_ref, hl_ref, state_ref, tail_ref, *, valid):
    c = pl.program_id(1)

    @pl.when(c == 0)
    def _():
        state_ref[...] = h0_ref[0]
        tail_ref[...] = hist_ref[0]

    xbc = xbc_ref[0]
    ext = jnp.concatenate([tail_ref[...], xbc], axis=0)
    acc = jnp.broadcast_to(cb_ref[...], xbc.shape)
    for w in range(CONV_WIDTH):
        lo = 8 - (CONV_WIDTH - 1) + w
        acc = acc + ext[lo:lo + BLK] * cw_ref[w:w + 1, :]
    tail_ref[...] = xbc[BLK - 8:]
    xc = _silu(acc)
    xs = xc[:, :D_MODEL]
    gn = N_GROUPS * D_STATE

    row = lax.broadcasted_iota(jnp.int32, (BLK, DT_PAD), 0)
    dt = dt_ref[0] + dtb_ref[...]
    dt = jnp.maximum(dt, 0.0) + jnp.log(1.0 + jnp.exp(-jnp.abs(dt)))
    if valid < BLK:
        dt = jnp.where(row < valid, dt, 0.0)
    dta = dt * (-jnp.exp(alog_ref[...]))
    ri = lax.broadcasted_iota(jnp.int32, (BLK, BLK), 0)
    ci = lax.broadcasted_iota(jnp.int32, (BLK, BLK), 1)
    causal = ri >= ci
    tri = jnp.where(causal, 1.0, 0.0)
    acs = jnp.dot(tri, dta, preferred_element_type=F32, precision=lax.Precision.HIGHEST)
    acs_t = acs.T
    dt_t = dt.T
    lane = lax.broadcasted_iota(jnp.int32, (1, BLK), 1)
    lo_half = lane < HEAD_DIM
    nt = (((1,), (1,)), ((), ()))
    hg = N_HEADS // N_GROUPS
    for g in range(N_GROUPS):
        bg = xc[:, D_MODEL + g * D_STATE:D_MODEL + (g + 1) * D_STATE].astype(BF16)
        cg = xc[:, D_MODEL + gn + g * D_STATE:D_MODEL + gn + (g + 1) * D_STATE].astype(BF16)
        cb = lax.dot_general(cg, bg, nt, preferred_element_type=F32)
        for pp in range(hg // 2):
            p = g * (hg // 2) + pp
            sl = slice(p * BLK, (p + 1) * BLK)
            x_pair = xs[:, sl]
            x_pair_b = x_pair.astype(BF16)
            y_pair = None
            dec_in = None
            to_end = None
            dec_out = None
            dsk = None
            for hh in range(2):
                h = 2 * p + hh
                hm = lo_half if hh == 0 else jnp.logical_not(lo_half)
                a_col = acs[:, h:h + 1]
                seg = a_col - acs_t[h:h + 1, :]
                lmat = jnp.exp(jnp.where(causal, seg, NEG_INF))
                mh = (cb * lmat * dt_t[h:h + 1, :]).astype(BF16)
                xm = jnp.where(hm, x_pair_b, jnp.zeros_like(x_pair_b))
                yh = jnp.dot(mh, xm, preferred_element_type=F32)
                y_pair = yh if y_pair is None else y_pair + yh
                a_last = acs[BLK - 1:BLK, h:h + 1]
                e_in = jnp.exp(a_col)
                e_end = jnp.exp(a_last - a_col) * dt[:, h:h + 1]
                e_out = jnp.exp(a_last)
                d_h = dskip_ref[h]
                if hh == 0:
                    dec_in, to_end, dec_out, dsk = e_in, e_end, e_out, d_h
                else:
                    dec_in = jnp.where(lo_half, dec_in, e_in)
                    to_end = jnp.where(lo_half, to_end, e_end)
                    dec_out = jnp.where(lo_half, dec_out, e_out)
                    dsk = jnp.where(lo_half, dsk, d_h)
            h_in = state_ref[sl, :]
            y_off = lax.dot_general(cg, h_in.astype(BF16), nt, preferred_element_type=F32)
            y_pair = y_pair + y_off * dec_in + x_pair * dsk
            xw_t = (x_pair * to_end).T.astype(BF16)
            st = jnp.dot(xw_t, bg, preferred_element_type=F32)
            state_ref[sl, :] = h_in * dec_out.T + st
            y_ref[0, :, sl] = y_pair

    u = y_ref[0] * _silu(z_ref[0])
    gs = D_MODEL // N_GROUPS
    parts = []
    for g in range(N_GROUPS):
        ug = u[:, g * gs:(g + 1) * gs]
        parts.append(ug * lax.rsqrt(jnp.mean(ug * ug, axis=-1, keepdims=True) + RMS_EPS))
    y_ref[0] = jnp.concatenate(parts, axis=1) * g_ref[...]

    @pl.when(c == pl.num_programs(1) - 1)
    def _():
        hl_ref[0] = state_ref[...]


def ssd_mixer(xbc, z, dt_raw, hist, h0, conv_w_t, conv_b, dt_bias, a_log, d_skip, norm_g, valid):
    b, l, _ = xbc.shape
    nc = l // BLK
    hp = N_HEADS * HEAD_DIM
    tok = lambda bi, c: (bi, c, 0)
    seq = lambda bi, c: (bi, 0, 0)
    const = lambda bi, c: (0, 0)
    return pl.pallas_call(
        functools.partial(_ssd_body, valid=valid),
        grid=(b, nc),
        in_specs=[pl.BlockSpec((1, BLK, CONV_DIM), tok), pl.BlockSpec((1, BLK, D_MODEL), tok),
                  pl.BlockSpec((1, BLK, DT_PAD), tok), pl.BlockSpec((1, 8, CONV_DIM), seq),
                  pl.BlockSpec((1, hp, D_STATE), seq),
                  pl.BlockSpec((CONV_WIDTH, CONV_DIM), const), pl.BlockSpec((1, CONV_DIM), const),
                  pl.BlockSpec((1, DT_PAD), const), pl.BlockSpec((1, DT_PAD), const),
                  pl.BlockSpec(memory_space=pltpu.SMEM), pl.BlockSpec((1, D_MODEL), const)],
        out_specs=[pl.BlockSpec((1, BLK, D_MODEL), tok), pl.BlockSpec((1, hp, D_STATE), seq)],
        out_shape=[jax.ShapeDtypeStruct((b, l, D_MODEL), F32), jax.ShapeDtypeStruct((b, hp, D_STATE), F32)],
        scratch_shapes=[pltpu.VMEM((hp, D_STATE), F32), pltpu.VMEM((8, CONV_DIM), F32)],
        compiler_params=_params("parallel", "arbitrary"),
        name="ssd_mixer",
    )(xbc, z, dt_raw, hist, h0, conv_w_t, conv_b, dt_bias, a_log, d_skip, norm_g)


def _outproj_body(*refs, n_cfg):
    o_refs = refs[:n_cfg]
    lse_refs = refs[n_cfg:2 * n_cfg] if n_cfg > 1 else ()
    ssm_ref, h_ref, w_ref, g_ref, b_ref, exp_ref, y_ref = refs[-7:]
    if n_cfg == 1:
        att = o_refs[0][...]
    else:
        lses = [r[...] for r in lse_refs]
        mx = functools.reduce(jnp.maximum, lses)
        es = [jnp.exp(l - mx) for l in lses]
        inv = 1.0 / functools.reduce(lambda a, c: a + c, es)
        att = None
        for o_ref, e in zip(o_refs, es):
            w = e * inv
            hi = w.astype(BF16)
            lo = (w - hi.astype(F32)).astype(BF16)
            wexp = (jnp.dot(hi, exp_ref[...], preferred_element_type=F32)
                    + jnp.dot(lo, exp_ref[...], preferred_element_type=F32))
            term = wexp * o_ref[...]
            att = term if att is None else att + term
    out = jnp.dot(att.astype(BF16), w_ref[:D_MODEL, :], preferred_element_type=F32)
    out = out + jnp.dot(ssm_ref[...].astype(BF16), w_ref[D_MODEL:, :], preferred_element_type=F32)
    y_ref[...] = _layer_norm(ALPHA * h_ref[...] + out, g_ref[...], b_ref[...])


def out_proj(att_parts, lse_parts, ssm, h, w_out_b, ln_g, ln_b, head_expand):
    m = h.shape[0]
    tm = min(256, m)
    n_cfg = len(att_parts)
    tok = lambda i: (i, 0)
    const = lambda i: (0, 0)
    ins = list(att_parts) + (list(lse_parts) if n_cfg > 1 else [])
    specs = [pl.BlockSpec((tm, D_MODEL), tok)] * n_cfg + ([pl.BlockSpec((tm, BLK), tok)] * n_cfg if n_cfg > 1 else [])
    return pl.pallas_call(
        functools.partial(_outproj_body, n_cfg=n_cfg),
        grid=(m // tm,),
        in_specs=specs + [pl.BlockSpec((tm, D_MODEL), tok), pl.BlockSpec((tm, D_MODEL), tok),
                          pl.BlockSpec((2 * D_MODEL, D_MODEL), const), pl.BlockSpec((1, D_MODEL), const),
                          pl.BlockSpec((1, D_MODEL), const), pl.BlockSpec((BLK, D_MODEL), const)],
        out_specs=pl.BlockSpec((tm, D_MODEL), tok),
        out_shape=jax.ShapeDtypeStruct((m, D_MODEL), F32),
        compiler_params=_params("parallel"),
        name=f"out_proj_c{n_cfg}",
    )(*ins, ssm, h, w_out_b, ln_g, ln_b, head_expand)


def _fold_body(sk_ref, wq_ref, o_ref):
    o_ref[0] = lax.dot_general(sk_ref[0], wq_ref[...], (((1,), (1,)), ((), ())),
                               preferred_element_type=F32, precision=lax.Precision.HIGHEST)


def fold_scores(peer_wq, peer_keys):
    dk = peer_keys.shape[-1]
    nblk = PEER_HEADS * 2
    folded = pl.pallas_call(
        _fold_body,
        grid=(nblk,),
        in_specs=[pl.BlockSpec((1, PEER_NKEYS, dk), lambda i: (i % 2, 0, 0)),
                  pl.BlockSpec((D_MODEL, dk), lambda i: (0, i))],
        out_specs=pl.BlockSpec((1, PEER_NKEYS, D_MODEL), lambda i: (i, 0, 0)),
        out_shape=jax.ShapeDtypeStruct((nblk, PEER_NKEYS, D_MODEL), F32),
        compiler_params=_params("parallel"),
        name="peer_fold",
    )(peer_keys, peer_wq)
    folded = folded.reshape(PEER_HEADS, 2, PEER_NKEYS, D_MODEL).transpose(1, 2, 0, 3)
    return folded.reshape(2 * PEER_NKEYS * PEER_HEADS, D_MODEL)


_CANDS = tuple((k, l) for k in range(PEER_TOPK) for l in range(PEER_TOPK) if (k + 1) * (l + 1) <= PEER_TOPK)


def _topk_extract(score_ref, work_ref, rank_ref, vals_ref):
    shape = work_ref.shape
    eio = lax.broadcasted_iota(jnp.int32, shape, 0)
    work_ref[...] = score_ref[...]
    rank_ref[...] = jnp.full(shape, 99.0, F32)

    def body(t, carry):
        sw = work_ref[...]
        m = jnp.max(sw, axis=0)
        idx = jnp.min(jnp.where(sw == m[None], eio, PEER_NKEYS), axis=0)
        sel = eio == idx[None]
        work_ref[...] = jnp.where(sel, NEG_INF, sw)
        rank_ref[...] = jnp.where(sel, t.astype(F32), rank_ref[...])
        vals_ref[t] = m
        return carry

    lax.fori_loop(0, PEER_TOPK, body, 0)


def _cmp_exchange(x, i, j):
    a, b = x[i], x[j]
    if b is None:
        return
    if a is None:
        x[i], x[j] = b, None
        return
    x[i], x[j] = jnp.maximum(a, b), jnp.minimum(a, b)


def _sort_desc(x):
    x = list(x)
    n = len(x)
    k = 2
    while k <= n:
        j = k // 2
        while j > 0:
            for i in range(n):
                l = i ^ j
                if l > i:
                    if (i & k) == 0:
                        _cmp_exchange(x, i, l)
                    else:
                        _cmp_exchange(x, l, i)
            j //= 2
        k *= 2
    return x


def _merge_top(a, b):
    n = len(a)
    m = []
    for i in range(n):
        u, v = a[i], b[n - 1 - i]
        m.append(v if u is None else (u if v is None else jnp.maximum(u, v)))
    j = n // 2
    while j > 0:
        for i in range(n):
            if (i & j) == 0:
                _cmp_exchange(m, i, i + j)
        j //= 2
    return m


def _top_sorted(tiles):
    groups = [_sort_desc(tiles[g:g + PEER_TOPK]) for g in range(0, len(tiles), PEER_TOPK)]
    while len(groups) > 1:
        groups = [_merge_top(groups[g], groups[g + 1]) for g in range(0, len(groups), 2)]
    return groups[0]


def _store_head_major(tmp_scr, val, out_ref):
    nk, nh = PEER_NKEYS, PEER_HEADS
    tmp_scr[...] = val.reshape(nk * nh, val.shape[-1])
    for h in range(nh):
        out_ref[h] = tmp_scr[pl.ds(h, nk, stride=nh), :].astype(BF16)


def _route_by_value(s_scr, cmap_ref, e1_ref, r2_ref, w2_ref, tmp_scr):
    topk = PEER_TOPK
    a = _top_sorted([s_scr[0, e] for e in range(PEER_NKEYS)])
    b = _top_sorted([s_scr[1, e] for e in range(PEER_NKEYS)])
    row_len = [topk // (k + 1) for k in range(topk)]
    sums = [[a[k] + b[l] for l in range(row_len[k])] for k in range(topk)]
    pad = lambda lst: list(lst) + [None] * (topk - len(lst))
    top = sums[0]
    lists = [sums[1], [sums[k][0] for k in range(topk // 2, topk)]] + [sums[k] for k in range(2, topk // 2)]
    for lst in lists:
        top = _merge_top(top, pad(lst))
    tau = top[topk - 1]
    z = None
    for t in range(topk):
        ez = jnp.exp(top[t] - top[0])
        z = ez if z is None else z + ez
    inv_z = 1.0 / z
    counts = []
    for k in range(topk):
        ck = None
        for l in range(row_len[k]):
            one = jnp.where(sums[k][l] >= tau, 1.0, 0.0)
            ck = one if ck is None else ck + one
        counts.append(ck)
    total = functools.reduce(lambda u, v: u + v, counts)
    floor1 = a[0]
    for k in range(1, topk):
        floor1 = jnp.where(counts[k] >= 1.0, a[k], floor1)

    s1 = s_scr[0]
    n1 = jnp.sum(jnp.where(s1 >= a[topk - 1][None], 1.0, 0.0), axis=0)
    cmap = jnp.where(s1 >= floor1[None], 1.0, 0.0)
    for k in range(topk // 2):
        cmap = jnp.where(s1 == a[k][None], counts[k][None], cmap)
    cmap_ref[...] = cmap
    e1_ref[...] = jnp.exp(s1 - a[0][None])
    s2 = s_scr[1]
    n2 = jnp.sum(jnp.where(s2 >= b[topk - 1][None], 1.0, 0.0), axis=0)
    rank2 = jnp.full(s2.shape, 99.0, F32)
    for l in range(topk):
        rank2 = jnp.where(s2 == b[l][None], float(l), rank2)
    _store_head_major(tmp_scr, rank2, r2_ref)
    _store_head_major(tmp_scr, jnp.exp(s2 - b[0][None]) * inv_z[None], w2_ref)
    ftop = float(topk)
    return ((total != ftop) | ((n1 > ftop) & (counts[topk - 1] >= 1.0)) | ((n2 > ftop) & (counts[0] >= ftop)))


def _route_body(h_ref, ws_ref, cmap_ref, e1_ref, r2_ref, w2_ref,
                s_scr, work_scr, rank1_scr, rank2_scr, a_scr, b_scr, tmp_scr):
    tl = s_scr.shape[-1]
    nk, nh = PEER_NKEYS, PEER_HEADS
    s_t = lax.dot_general(ws_ref[...], h_ref[...].astype(BF16), (((1,), (1,)), ((), ())),
                          preferred_element_type=F32)
    s_scr[...] = s_t.reshape(2, nk, nh, tl)
    unsure = _route_by_value(s_scr, cmap_ref, e1_ref, r2_ref, w2_ref, tmp_scr)

    @pl.when(jnp.max(jnp.where(unsure, 1.0, 0.0)) > 0.0)
    def _():
        _route_by_index(s_scr, cmap_ref, e1_ref, r2_ref, w2_ref,
                        work_scr, rank1_scr, rank2_scr, a_scr, b_scr, tmp_scr)


def _route_by_index(s_scr, cmap_ref, e1_ref, r2_ref, w2_ref,
                    work_scr, rank1_scr, rank2_scr, a_scr, b_scr, tmp_scr):
    _topk_extract(s_scr.at[0], work_scr, rank1_scr, a_scr)
    _topk_extract(s_scr.at[1], work_scr, rank2_scr, b_scr)

    cvals = [a_scr[k] + b_scr[l] for k, l in _CANDS]
    nc = len(_CANDS)
    after = [None] * nc
    before = [None] * nc
    for x in range(nc):
        for y in range(x + 1, nc):
            gt = jnp.where(cvals[y] > cvals[x], 1.0, 0.0)
            after[x] = gt if after[x] is None else after[x] + gt
            before[y] = gt if before[y] is None else before[y] + gt
    z = None
    counts = [None] * PEER_TOPK
    for x, (k, l) in enumerate(_CANDS):
        ahead = float(x)
        if after[x] is not None:
            ahead = ahead + after[x]
        if before[x] is not None:
            ahead = ahead - before[x]
        chosen = ahead < float(PEER_TOPK)
        one = jnp.where(chosen, 1.0, 0.0)
        counts[k] = one if counts[k] is None else counts[k] + one
        ez = jnp.where(chosen, jnp.exp(cvals[x] - cvals[0]), 0.0)
        z = ez if z is None else z + ez
    inv_z = 1.0 / z

    rank1 = rank1_scr[...]
    cmap = jnp.zeros(rank1.shape, F32)
    for k in range(PEER_TOPK):
        cmap = jnp.where(rank1 == float(k), counts[k][None], cmap)
    cmap_ref[...] = cmap
    e1_ref[...] = jnp.where(rank1 < float(PEER_TOPK), jnp.exp(s_scr[0] - a_scr[0][None]), 0.0)
    rank2 = rank2_scr[...]
    w2 = jnp.where(rank2 < float(PEER_TOPK), jnp.exp(s_scr[1] - b_scr[0][None]) * inv_z[None], 0.0)
    _store_head_major(tmp_scr, rank2, r2_ref)
    _store_head_major(tmp_scr, w2, w2_ref)


def peer_route(h2d, ws_t):
    m = h2d.shape[0]
    tl = 128
    nk, nh = PEER_NKEYS, PEER_HEADS
    tok3 = lambda i: (0, 0, i)
    return pl.pallas_call(
        _route_body,
        grid=(m // tl,),
        in_specs=[pl.BlockSpec((tl, D_MODEL), lambda i: (i, 0)),
                  pl.BlockSpec((2 * nk * nh, D_MODEL), lambda i: (0, 0))],
        out_specs=[pl.BlockSpec((nk, nh, tl), tok3), pl.BlockSpec((nk, nh, tl), tok3),
                   pl.BlockSpec((nh, nk, tl), tok3), pl.BlockSpec((nh, nk, tl), tok3)],
        out_shape=[jax.ShapeDtypeStruct((nk, nh, m), F32), jax.ShapeDtypeStruct((nk, nh, m), F32),
                   jax.ShapeDtypeStruct((nh, nk, m), BF16), jax.ShapeDtypeStruct((nh, nk, m), BF16)],
        scratch_shapes=[pltpu.VMEM((2, nk, nh, tl), F32), pltpu.VMEM((nk, nh, tl), F32),
                        pltpu.VMEM((nk, nh, tl), F32), pltpu.VMEM((nk, nh, tl), F32),
                        pltpu.VMEM((PEER_TOPK, nh, tl), F32), pltpu.VMEM((PEER_TOPK, nh, tl), F32),
                        pltpu.VMEM((nk * nh, tl), F32)],
        compiler_params=_params("parallel"),
        name="peer_route",
    )(h2d, ws_t)


def _gated_hidden(i, ib, a_scr, hid_scr, cmap_ref, e1_ref, r2_ref, w2_ref):
    tm = a_scr.shape[-1]
    sub = 16
    grp = (PEER_NKEYS // sub, sub, tm)
    blk = slice(ib * PEER_NKEYS, (ib + 1) * PEER_NKEYS)
    gate = jnp.zeros(grp, BF16)
    for h in range(PEER_HEADS):
        cm = jnp.broadcast_to(cmap_ref[i, h:h + 1, :], (sub, tm)).astype(BF16)
        e1 = jnp.broadcast_to(e1_ref[i, h:h + 1, :], (sub, tm)).astype(BF16)
        w2 = w2_ref[h].reshape(grp)
        gate = gate + jnp.where(r2_ref[h].reshape(grp) < cm[None], w2, jnp.zeros_like(w2)) * e1[None]
    a = a_scr[blk, :]
    act = 0.5 * a * (1.0 + lax.erf(a * (2.0 ** -0.5)))
    hid_scr[blk, :] = gate.reshape(PEER_NKEYS, tm) * act.astype(BF16)


def _experts_body(h_ref, u_ref, vt_ref, cmap_ref, e1_ref, r2_ref, w2_ref, g_ref, b_ref, y_ref,
                  acc_ref, a_scr, hid_scr, ht_scr, *, rows):
    j = pl.program_id(1)

    @pl.when(j == 0)
    def _():
        acc_ref[...] = jnp.zeros_like(acc_ref)
        ht_scr[...] = h_ref[...].T.astype(BF16)

    a_scr[...] = jnp.dot(u_ref[...], ht_scr[...], preferred_element_type=F32)
    for ib in range(rows):
        _gated_hidden(ib, ib, a_scr, hid_scr, cmap_ref, e1_ref, r2_ref, w2_ref)
    acc_ref[...] += jnp.dot(vt_ref[...], hid_scr[...], preferred_element_type=F32)

    @pl.when(j == pl.num_programs(1) - 1)
    def _():
        y_ref[...] = _layer_norm(ALPHA * h_ref[...] + acc_ref[...].T, g_ref[...], b_ref[...])


def peer_experts(h2d, u_b, vt_b, cmap, e1map, rank2, w2, ln_g, ln_b):
    m = h2d.shape[0]
    tm = min(512, m)
    te = 1024
    rows = te // PEER_NKEYS
    nk, nh = PEER_NKEYS, PEER_HEADS
    tok = lambda i, j: (i, 0)
    tok3 = lambda i, j: (0, 0, i)
    const = lambda i, j: (0, 0)
    return pl.pallas_call(
        functools.partial(_experts_body, rows=rows),
        grid=(m // tm, u_b.shape[0] // te),
        in_specs=[pl.BlockSpec((tm, D_MODEL), tok),
                  pl.BlockSpec((te, D_MODEL), lambda i, j: (j, 0)),
                  pl.BlockSpec((D_MODEL, te), lambda i, j: (0, j)),
                  pl.BlockSpec((rows, nh, tm), lambda i, j: (j, 0, i)),
                  pl.BlockSpec((rows, nh, tm), lambda i, j: (j, 0, i)),
                  pl.BlockSpec((nh, nk, tm), tok3), pl.BlockSpec((nh, nk, tm), tok3),
                  pl.BlockSpec((1, D_MODEL), const), pl.BlockSpec((1, D_MODEL), const)],
        out_specs=pl.BlockSpec((tm, D_MODEL), tok),
        out_shape=jax.ShapeDtypeStruct((m, D_MODEL), F32),
        scratch_shapes=[pltpu.VMEM((D_MODEL, tm), F32), pltpu.VMEM((te, tm), F32), pltpu.VMEM((te, tm), BF16),
                        pltpu.VMEM((D_MODEL, tm), BF16)],
        compiler_params=_params("parallel", "arbitrary"),
        name="peer_experts",
    )(h2d, u_b, vt_b, cmap, e1map, rank2, w2, ln_g, ln_b)


def _transpose_body(x_ref, o_ref):
    o_ref[...] = x_ref[...].T.astype(BF16)


def transpose_cast(x):
    n, d = x.shape
    tn = 512
    return pl.pallas_call(
        _transpose_body,
        grid=(n // tn,),
        in_specs=[pl.BlockSpec((tn, d), lambda i: (i, 0))],
        out_specs=pl.BlockSpec((d, tn), lambda i: (0, i)),
        out_shape=jax.ShapeDtypeStruct((d, n), BF16),
        compiler_params=_params("parallel"),
        name="transpose_cast",
    )(x)


def peer_layer(h2d, ws_t, u_b, vt_b, ln_g, ln_b):
    cmap, e1map, rank2, w2 = peer_route(h2d, ws_t)
    return peer_experts(h2d, u_b, vt_b, cmap, e1map, rank2, w2, ln_g, ln_b)


def kernel(x_prompt, x_sample, cache_k, cache_v, state_conv, state_ssm, w_in, conv_w, conv_b, dt_bias, a_log,
           d_skip, ssm_norm_g, w_out, ln1_g, ln1_b, peer_wq, peer_keys, peer_u, peer_v, ln2_g, ln2_b):
    bp, sp, d = x_prompt.shape
    bs, ts, _ = x_sample.shape
    w_buf = cache_k.shape[2]
    hp = N_HEADS * HEAD_DIM
    tq = 8
    hpv = x_prompt.reshape(bp * sp, d)
    hsv = x_sample.reshape(bs * ts, d)
    lane_head = jnp.arange(D_MODEL)[None, :] // HEAD_DIM
    head_expand = (jnp.arange(BLK)[:, None] == lane_head).astype(BF16)
    pad_lanes = lambda a: jnp.pad(a[None, :], ((0, 0), (0, DT_PAD - a.shape[0])))
    outs = {n: [] for n in ("pk", "pv", "pc", "ps", "sc", "ss")}
    assert w_buf == MAX_WINDOW, "the new window drops exactly the ts oldest keys"
    to_feature_major = lambda a: jnp.transpose(a, (0, 1, 3, 4, 2)).reshape(DEPTH, bs, d, w_buf)
    from_feature_major = lambda a: jnp.transpose(a.reshape(DEPTH, bs, N_HEADS, HEAD_DIM, w_buf), (0, 1, 4, 2, 3))
    cache_kt, cache_vt = to_feature_major(cache_k), to_feature_major(cache_v)
    new_kt = new_vt = None

    for l in range(DEPTH):
        w_pad = jnp.pad(w_in[l], ((0, 0), (0, DT_PAD - N_HEADS))).astype(BF16)
        w_out_b = w_out[l].astype(BF16)
        cw_t = conv_w[l].T
        cb = conv_b[l][None, :]
        dtb, alog = pad_lanes(dt_bias[l]), pad_lanes(a_log[l])
        ng = ssm_norm_g[l][None, :]
        g1, b1, g2, b2 = ln1_g[l][None, :], ln1_b[l][None, :], ln2_g[l][None, :], ln2_b[l][None, :]
        ws_t = fold_scores(peer_wq[l], peer_keys[l]).astype(BF16)
        u_b = peer_u[l].astype(BF16)
        vt_b = transpose_cast(peer_v[l])

        q, k, v, z, xbc, dtr = in_proj(hpv, w_pad)
        q3, k3, v3 = (a.reshape(bp, sp, d) for a in (q, k, v))
        att = [band_attention(q3, k3, v3, dil) for _, dil in DILATED_CONFIGS]
        y, h_last = ssd_mixer(xbc.reshape(bp, sp, CONV_DIM), z.reshape(bp, sp, d), dtr.reshape(bp, sp, DT_PAD),
                              jnp.zeros((bp, 8, CONV_DIM), F32), jnp.zeros((bp, hp, D_STATE), F32),
                              cw_t, cb, dtb, alog, d_skip[l], ng, valid=BLK)
        hpv = out_proj([o.reshape(bp * sp, d) for o, _ in att], [s.reshape(bp * sp, BLK) for _, s in att],
                       y.reshape(bp * sp, d), hpv, w_out_b, g1, b1, head_expand)
        hpv = peer_layer(hpv, ws_t, u_b, vt_b, g2, b2)
        keep = max(0, sp - MAX_WINDOW)
        outs["pk"].append(k3[:, keep:].reshape(bp, sp - keep, N_HEADS, HEAD_DIM))
        outs["pv"].append(v3[:, keep:].reshape(bp, sp - keep, N_HEADS, HEAD_DIM))
        outs["pc"].append(xbc.reshape(bp, sp, CONV_DIM)[:, sp - (CONV_WIDTH - 1):])
        outs["ps"].append(h_last.reshape(bp, N_HEADS, HEAD_DIM, D_STATE))

        q, k, v, z, xbc, dtr = in_proj(hsv, w_pad)
        pad_t = lambda a, n: jnp.pad(a.reshape(bs, ts, a.shape[-1]), ((0, 0), (0, n - ts), (0, 0)))
        att_s, new_kt, new_vt = cache_attention(pad_t(q, tq), pad_t(k, tq), pad_t(v, tq), cache_kt, cache_vt, l,
                                                None if l == 0 else (new_kt, new_vt), ts)
        att_s = att_s[:, :ts].reshape(bs * ts, d)
        hist = jnp.pad(state_conv[l], ((0, 0), (8 - (CONV_WIDTH - 1), 0), (0, 0)))
        y, h_last = ssd_mixer(pad_t(xbc, BLK), pad_t(z, BLK), pad_t(dtr, BLK), hist,
                              state_ssm[l].reshape(bs, hp, D_STATE), cw_t, cb, dtb, alog, d_skip[l], ng, valid=ts)
        y = y[:, :ts].reshape(bs * ts, d)
        hsv = out_proj([att_s], [], y, hsv, w_out_b, g1, b1, head_expand)
        hsv = peer_layer(hsv, ws_t, u_b, vt_b, g2, b2)
        conv_ext = jnp.concatenate([state_conv[l], xbc.reshape(bs, ts, CONV_DIM)], axis=1)
        outs["sc"].append(conv_ext[:, -(CONV_WIDTH - 1):])
        outs["ss"].append(h_last.reshape(bs, N_HEADS, HEAD_DIM, D_STATE))

    st = {n: jnp.stack(v) for n, v in outs.items()}
    return (hpv.reshape(bp, sp, d), hsv.reshape(bs, ts, d), st["pk"], st["pv"], st["pc"], st["ps"],
            from_feature_major(new_kt), from_feature_major(new_vt), st["sc"], st["ss"])
```

```python
import functools
import math

import jax
import jax.numpy as jnp
from jax import lax
from jax.experimental import pallas as pl
from jax.experimental.pallas import tpu as pltpu

F32 = jnp.float32
BF16 = jnp.bfloat16

D_MODEL = 1024
DEPTH = 2
HEAD_DIM = 64
N_HEADS = 16
N_GROUPS = 2
D_STATE = 128
CONV_WIDTH = 4
CONV_DIM = D_MODEL + 2 * N_GROUPS * D_STATE
DILATED_CONFIGS = ((128, 1), (512, 4), (2048, 16))
MAX_WINDOW = 2048
BLK = 128
PEER_HEADS = 8
PEER_NKEYS = 128
PEER_TOPK = 16
LN_EPS = 1e-5
RMS_EPS = 1e-5
ALPHA = (2 * DEPTH) ** 0.25
DT_PAD = 128
PROJ_WIDTHS = (D_MODEL, D_MODEL, D_MODEL, D_MODEL, CONV_DIM, DT_PAD)
VMEM_LIMIT = 56 * 1024 * 1024
SLOPES = tuple(2.0 ** (-8.0 * (h + 1) / N_HEADS) for h in range(N_HEADS))
NEG_INF = float("-inf")


def _params(*sem, flags=None):
    return pltpu.CompilerParams(dimension_semantics=sem, vmem_limit_bytes=VMEM_LIMIT, flags=flags)


def _layer_norm(x, g, b):
    mu = jnp.mean(x, axis=-1, keepdims=True)
    xc = x - mu
    var = jnp.mean(xc * xc, axis=-1, keepdims=True)
    return xc * lax.rsqrt(var + LN_EPS) * g + b


def _silu(x):
    return x * (1.0 / (1.0 + jnp.exp(-x)))


def _inproj_body(x_ref, w_ref, *out_refs):
    x = x_ref[...].astype(BF16)
    off = 0
    for ref in out_refs:
        n = ref.shape[-1]
        ref[...] = jnp.dot(x, w_ref[:, off:off + n], preferred_element_type=F32)
        off += n


def in_proj(x2d, w_pad):
    m = x2d.shape[0]
    tm = min(256, m)
    n_all = sum(PROJ_WIDTHS)
    return pl.pallas_call(
        _inproj_body,
        grid=(m // tm,),
        in_specs=[pl.BlockSpec((tm, D_MODEL), lambda i: (i, 0)),
                  pl.BlockSpec((D_MODEL, n_all), lambda i: (0, 0))],
        out_specs=[pl.BlockSpec((tm, n), lambda i: (i, 0)) for n in PROJ_WIDTHS],
        out_shape=[jax.ShapeDtypeStruct((m, n), F32) for n in PROJ_WIDTHS],
        compiler_params=_params("parallel"),
        name="in_proj",
    )(x2d, w_pad)


def _pair_attention(qp, kp, vp, valid, negdist, lane, slopes, heads, lse_tile):
    lo = lane < HEAD_DIM
    zero = jnp.zeros_like(qp)
    q2 = jnp.concatenate([jnp.where(lo, qp, zero), jnp.where(lo, zero, qp)], axis=0)
    s2 = lax.dot_general(q2, kp, (((1,), (1,)), ((), ())), preferred_element_type=F32)
    probs = []
    for hh in range(2):
        s = jnp.where(valid, s2[hh * BLK:(hh + 1) * BLK] + slopes[hh] * negdist, NEG_INF)
        mx = jnp.max(s, axis=1, keepdims=True)
        e = jnp.exp(s - mx)
        l = jnp.sum(e, axis=1, keepdims=True)
        probs.append((e * (1.0 / l)).astype(BF16))
        lse_tile = jnp.where(lane == heads[hh], mx + jnp.log(l), lse_tile)
    o2 = jnp.dot(jnp.concatenate(probs, axis=0), vp, preferred_element_type=F32)
    return jnp.where(lo, o2[:BLK], o2[BLK:]), lse_tile


def _band_attn_body(q_ref, kp_ref, kc_ref, vp_ref, vc_ref, o_ref, lse_ref, *, dil):
    n = pl.program_id(2)
    q = (q_ref[0] * (HEAD_DIM ** -0.5)).astype(BF16)
    k2 = jnp.concatenate([kp_ref[0], kc_ref[0]], axis=0).astype(BF16)
    v2 = jnp.concatenate([vp_ref[0], vc_ref[0]], axis=0).astype(BF16)
    qi = lax.broadcasted_iota(jnp.int32, (BLK, 2 * BLK), 0)
    kj = lax.broadcasted_iota(jnp.int32, (BLK, 2 * BLK), 1)
    back = BLK + qi - kj
    valid = (back >= 0) & (back <= BLK) & ((kj >= BLK) | (n > 0))
    negdist = -(back * dil).astype(F32)
    lane = lax.broadcasted_iota(jnp.int32, (1, BLK), 1)
    lse_all = jnp.zeros((BLK, BLK), F32)
    for p in range(N_HEADS // 2):
        sl = slice(p * BLK, (p + 1) * BLK)
        acc, lse_all = _pair_attention(q[:, sl], k2[:, sl], v2[:, sl], valid, negdist, lane,
                                       (SLOPES[2 * p], SLOPES[2 * p + 1]), (2 * p, 2 * p + 1), lse_all)
        o_ref[0, :, sl] = acc
    lse_ref[0] = lse_all


def _band_attn_strided_body(slope_ref, q_ref, kp_ref, kc_ref, vp_ref, vc_ref, o_ref, lse_ref, *, dil):
    n = pl.program_id(1)
    p = pl.program_id(2)

    @pl.when(p == 0)
    def _():
        lse_ref[...] = jnp.zeros_like(lse_ref)

    qi = lax.broadcasted_iota(jnp.int32, (BLK, 2 * BLK), 0)
    kj = lax.broadcasted_iota(jnp.int32, (BLK, 2 * BLK), 1)
    back = BLK + qi - kj
    valid = (back >= 0) & (back <= BLK) & ((kj >= BLK) | (n > 0))
    negdist = -(back * dil).astype(F32)
    lane = lax.broadcasted_iota(jnp.int32, (1, BLK), 1)
    for r in range(dil):
        rows = pl.ds(r, BLK, stride=dil)
        qp = (q_ref[0, rows, :] * (HEAD_DIM ** -0.5)).astype(BF16)
        kp = jnp.concatenate([kp_ref[0, rows, :], kc_ref[0, rows, :]], axis=0).astype(BF16)
        vp = jnp.concatenate([vp_ref[0, rows, :], vc_ref[0, rows, :]], axis=0).astype(BF16)
        acc, lse_rows = _pair_attention(qp, kp, vp, valid, negdist, lane,
                                        (slope_ref[2 * p], slope_ref[2 * p + 1]), (2 * p, 2 * p + 1),
                                        lse_ref[0, rows, :])
        o_ref[0, rows, :] = acc
        lse_ref[0, rows, :] = lse_rows


def band_attention(q, k, v, dil):
    b, s, d = q.shape
    if dil == 1:
        nblk = s // BLK
        cur = lambda bi, r, n: (bi, n, r)
        prev = lambda bi, r, n: (bi, jnp.maximum(n - 1, 0), r)
        return pl.pallas_call(
            functools.partial(_band_attn_body, dil=dil),
            grid=(b, 1, nblk),
            in_specs=[pl.BlockSpec((1, BLK, d), cur),
                      pl.BlockSpec((1, BLK, d), prev), pl.BlockSpec((1, BLK, d), cur),
                      pl.BlockSpec((1, BLK, d), prev), pl.BlockSpec((1, BLK, d), cur)],
            out_specs=[pl.BlockSpec((1, BLK, d), cur), pl.BlockSpec((1, BLK, BLK), cur)],
            out_shape=[jax.ShapeDtypeStruct((b, s, d), F32), jax.ShapeDtypeStruct((b, s, BLK), F32)],
            compiler_params=_params("parallel", "parallel", "arbitrary"),
            name="band_attn_d1",
        )(q, k, k, v, v)
    unit = BLK * dil
    cur = lambda bi, n, p: (bi, n, p)
    prev = lambda bi, n, p: (bi, jnp.maximum(n - 1, 0), p)
    return pl.pallas_call(
        functools.partial(_band_attn_strided_body, dil=dil),
        grid=(b, s // unit, d // BLK),
        in_specs=[pl.BlockSpec(memory_space=pltpu.SMEM),
                  pl.BlockSpec((1, unit, BLK), cur),
                  pl.BlockSpec((1, unit, BLK), prev), pl.BlockSpec((1, unit, BLK), cur),
                  pl.BlockSpec((1, unit, BLK), prev), pl.BlockSpec((1, unit, BLK), cur)],
        out_specs=[pl.BlockSpec((1, unit, BLK), cur), pl.BlockSpec((1, unit, BLK), lambda bi, n, p: (bi, n, 0))],
        out_shape=[jax.ShapeDtypeStruct((b, s, d), F32), jax.ShapeDtypeStruct((b, s, BLK), F32)],
        compiler_params=_params("parallel", "parallel", "arbitrary"),
        name=f"band_attn_d{dil}",
    )(jnp.asarray(SLOPES, F32), q, k, k, v, v)


HEADS_PER_STEP = 4


def _cache_attn_body(slope_ref, q_ref, kn_ref, vn_ref, ck_ref, cv_ref, *rest, t_new):
    o_ref, nk_ref, nv_ref = rest[-3:]
    c = pl.program_id(1)
    fc, w_buf = ck_ref.shape[2], ck_ref.shape[3]
    tp = q_ref.shape[1]
    hc = fc // HEAD_DIM
    nrow = tp * hc
    hrow = lax.broadcasted_iota(jnp.int32, (hc, fc), 0)
    hlane = lax.broadcasted_iota(jnp.int32, (hc, fc), 1) // HEAD_DIM
    blockmask = hrow == hlane
    q = q_ref[0] * (HEAD_DIM ** -0.5)
    qbd = jnp.concatenate(
        [jnp.where(blockmask, jnp.broadcast_to(q[t:t + 1], (hc, fc)), 0.0) for t in range(tp)],
        axis=0).astype(BF16)
    k_t = ck_ref[0, 0]
    v_t = cv_ref[0, 0]
    kn = kn_ref[0]
    vn = vn_ref[0]
    nt = (((1,), (1,)), ((), ()))
    s_c = jnp.dot(qbd, k_t.astype(BF16), preferred_element_type=F32)
    s_n = lax.dot_general(qbd, kn.astype(BF16), nt, preferred_element_type=F32)
    rix = lax.broadcasted_iota(jnp.int32, (nrow, 1), 0)
    row_t = rix // hc
    slope = jnp.zeros((nrow, 1), F32)
    for hh in range(hc):
        slope = jnp.where(rix % hc == hh, slope_ref[c * hc + hh], slope)

    def weights(s, key_pos):
        dist = (w_buf + row_t) - key_pos
        mult = jnp.zeros(s.shape, F32)
        for window, dil in DILATED_CONFIGS:
            hit = (dist >= 0) & (dist <= window) & ((dist & (dil - 1)) == 0)
            mult = mult + jnp.where(hit, 1.0, 0.0)
        sb = jnp.where(mult > 0.0, s - slope * dist.astype(F32), NEG_INF)
        return sb, mult

    sb_c, mult_c = weights(s_c, lax.broadcasted_iota(jnp.int32, s_c.shape, 1))
    sb_n, mult_n = weights(s_n, w_buf + lax.broadcasted_iota(jnp.int32, s_n.shape, 1))
    mx = jnp.maximum(jnp.max(sb_c, axis=1, keepdims=True), jnp.max(sb_n, axis=1, keepdims=True))
    p_c = mult_c * jnp.exp(sb_c - mx)
    p_n = mult_n * jnp.exp(sb_n - mx)
    inv = 1.0 / (jnp.sum(p_c, axis=1, keepdims=True) + jnp.sum(p_n, axis=1, keepdims=True))
    o_all = lax.dot_general((p_c * inv).astype(BF16), v_t.astype(BF16), nt, preferred_element_type=F32)
    o_all = o_all + jnp.dot((p_n * inv).astype(BF16), vn.astype(BF16), preferred_element_type=F32)
    rows = [jnp.sum(jnp.where(blockmask, o_all[t * hc:(t + 1) * hc], 0.0), axis=0, keepdims=True)
            for t in range(tp)]
    o_ref[0] = jnp.concatenate(rows, axis=0)

    lane = lax.broadcasted_iota(jnp.int32, (1, BLK), 1)

    def shifted(old_t, new_rows, out_ref):
        rolled = pltpu.roll(old_t, w_buf - t_new, 1)
        new_t = new_rows.T
        last = rolled[:, w_buf - BLK:]
        for t in range(t_new):
            last = jnp.where(lane == BLK - t_new + t, new_t[:, t:t + 1], last)
        out_ref[0, 0, :, :w_buf - BLK] = rolled[:, :w_buf - BLK]
        out_ref[0, 0, :, w_buf - BLK:] = last

    shifted(k_t, kn, nk_ref)
    shifted(v_t, vn, nv_ref)


def cache_attention(q, k_new, v_new, cache_kt, cache_vt, layer, prev, t_new):
    b, tp, d = q.shape
    w_buf = cache_kt.shape[-1]
    fc = HEADS_PER_STEP * HEAD_DIM
    tok = lambda i, c: (i, 0, c)
    win = lambda i, c: (layer, i, c, 0)
    ins = [jnp.asarray(SLOPES, F32), q, k_new, v_new, cache_kt, cache_vt]
    specs = [pl.BlockSpec(memory_space=pltpu.SMEM),
             pl.BlockSpec((1, tp, fc), tok), pl.BlockSpec((1, tp, fc), tok), pl.BlockSpec((1, tp, fc), tok),
             pl.BlockSpec((1, 1, fc, w_buf), win), pl.BlockSpec((1, 1, fc, w_buf), win)]
    aliases = {}
    if prev is not None:
        aliases = {len(ins): 1, len(ins) + 1: 2}
        ins += list(prev)
        specs += [pl.BlockSpec(memory_space=pl.ANY)] * 2
    win_shape = jax.ShapeDtypeStruct(cache_kt.shape, F32)
    return pl.pallas_call(
        functools.partial(_cache_attn_body, t_new=t_new),
        grid=(b, d // fc),
        in_specs=specs,
        out_specs=[pl.BlockSpec((1, tp, fc), tok), pl.BlockSpec((1, 1, fc, w_buf), win),
                   pl.BlockSpec((1, 1, fc, w_buf), win)],
        out_shape=[jax.ShapeDtypeStruct((b, tp, d), F32), win_shape, win_shape],
        input_output_aliases=aliases,
        compiler_params=_params("parallel", "parallel"),
        name="cache_attn",
    )(*ins)


def _ssd_body(xbc_ref, z_ref, dt_ref, hist_ref, h0_ref, cw_ref, cb_ref, dtb_ref, alog_ref, dskip_ref, g_ref,
              y_ref, hl_ref, state_ref, tail_ref, *, valid):
    c = pl.program_id(1)

    @pl.when(c == 0)
    def _():
        state_ref[...] = h0_ref[0]
        tail_ref[...] = hist_ref[0]

    xbc = xbc_ref[0]
    ext = jnp.concatenate([tail_ref[...], xbc], axis=0)
    acc = jnp.broadcast_to(cb_ref[...], xbc.shape)
    for w in range(CONV_WIDTH):
        lo = 8 - (CONV_WIDTH - 1) + w
        acc = acc + ext[lo:lo + BLK] * cw_ref[w:w + 1, :]
    tail_ref[...] = xbc[BLK - 8:]
    xc = _silu(acc)
    xs = xc[:, :D_MODEL]
    gn = N_GROUPS * D_STATE

    row = lax.broadcasted_iota(jnp.int32, (BLK, DT_PAD), 0)
    dt = dt_ref[0] + dtb_ref[...]
    dt = jnp.maximum(dt, 0.0) + jnp.log(1.0 + jnp.exp(-jnp.abs(dt)))
    if valid < BLK:
        dt = jnp.where(row < valid, dt, 0.0)
    dta = dt * (-jnp.exp(alog_ref[...]))
    ri = lax.broadcasted_iota(jnp.int32, (BLK, BLK), 0)
    ci = lax.broadcasted_iota(jnp.int32, (BLK, BLK), 1)
    causal = ri >= ci
    tri = jnp.where(causal, 1.0, 0.0)
    acs = jnp.dot(tri, dta, preferred_element_type=F32, precision=lax.Precision.HIGHEST)
    acs_t = acs.T
    dt_t = dt.T
    lane = lax.broadcasted_iota(jnp.int32, (1, BLK), 1)
    lo_half = lane < HEAD_DIM
    nt = (((1,), (1,)), ((), ()))
    hg = N_HEADS // N_GROUPS
    for g in range(N_GROUPS):
        bg = xc[:, D_MODEL + g * D_STATE:D_MODEL + (g + 1) * D_STATE].astype(BF16)
        cg = xc[:, D_MODEL + gn + g * D_STATE:D_MODEL + gn + (g + 1) * D_STATE].astype(BF16)
        cb = lax.dot_general(cg, bg, nt, preferred_element_type=F32)
        for pp in range(hg // 2):
            p = g * (hg // 2) + pp
            sl = slice(p * BLK, (p + 1) * BLK)
            x_pair = xs[:, sl]
            x_pair_b = x_pair.astype(BF16)
            y_pair = None
            dec_in = None
            to_end = None
            dec_out = None
            dsk = None
            for hh in range(2):
                h = 2 * p + hh
                hm = lo_half if hh == 0 else jnp.logical_not(lo_half)
                a_col = acs[:, h:h + 1]
                seg = a_col - acs_t[h:h + 1, :]
                lmat = jnp.exp(jnp.where(causal, seg, NEG_INF))
                mh = (cb * lmat * dt_t[h:h + 1, :]).astype(BF16)
                xm = jnp.where(hm, x_pair_b, jnp.zeros_like(x_pair_b))
                yh = jnp.dot(mh, xm, preferred_element_type=F32)
                y_pair = yh if y_pair is None else y_pair + yh
                a_last = acs[BLK - 1:BLK, h:h + 1]
                e_in = jnp.exp(a_col)
                e_end = jnp.exp(a_last - a_col) * dt[:, h:h + 1]
                e_out = jnp.exp(a_last)
                d_h = dskip_ref[h]
                if hh == 0:
                    dec_in, to_end, dec_out, dsk = e_in, e_end, e_out, d_h
                else:
                    dec_in = jnp.where(lo_half, dec_in, e_in)
                    to_end = jnp.where(lo_half, to_end, e_end)
                    dec_out = jnp.where(lo_half, dec_out, e_out)
                    dsk = jnp.where(lo_half, dsk, d_h)
            h_in = state_ref[sl, :]
            y_off = lax.dot_general(cg, h_in.astype(BF16), nt, preferred_element_type=F32)
            y_pair = y_pair + y_off * dec_in + x_pair * dsk
            xw_t = (x_pair * to_end).T.astype(BF16)
            st = jnp.dot(xw_t, bg, preferred_element_type=F32)
            state_ref[sl, :] = h_in * dec_out.T + st
            y_ref[0, :, sl] = y_pair

    u = y_ref[0] * _silu(z_ref[0])
    gs = D_MODEL // N_GROUPS
    parts = []
    for g in range(N_GROUPS):
        ug = u[:, g * gs:(g + 1) * gs]
        parts.append(ug * lax.rsqrt(jnp.mean(ug * ug, axis=-1, keepdims=True) + RMS_EPS))
    y_ref[0] = jnp.concatenate(parts, axis=1) * g_ref[...]

    @pl.when(c == pl.num_programs(1) - 1)
    def _():
        hl_ref[0] = state_ref[...]


def ssd_mixer(xbc, z, dt_raw, hist, h0, conv_w_t, conv_b, dt_bias, a_log, d_skip, norm_g, valid):
    b, l, _ = xbc.shape
    nc = l // BLK
    hp = N_HEADS * HEAD_DIM
    tok = lambda bi, c: (bi, c, 0)
    seq = lambda bi, c: (bi, 0, 0)
    const = lambda bi, c: (0, 0)
    return pl.pallas_call(
        functools.partial(_ssd_body, valid=valid),
        grid=(b, nc),
        in_specs=[pl.BlockSpec((1, BLK, CONV_DIM), tok), pl.BlockSpec((1, BLK, D_MODEL), tok),
                  pl.BlockSpec((1, BLK, DT_PAD), tok), pl.BlockSpec((1, 8, CONV_DIM), seq),
                  pl.BlockSpec((1, hp, D_STATE), seq),
                  pl.BlockSpec((CONV_WIDTH, CONV_DIM), const), pl.BlockSpec((1, CONV_DIM), const),
                  pl.BlockSpec((1, DT_PAD), const), pl.BlockSpec((1, DT_PAD), const),
                  pl.BlockSpec(memory_space=pltpu.SMEM), pl.BlockSpec((1, D_MODEL), const)],
        out_specs=[pl.BlockSpec((1, BLK, D_MODEL), tok), pl.BlockSpec((1, hp, D_STATE), seq)],
        out_shape=[jax.ShapeDtypeStruct((b, l, D_MODEL), F32), jax.ShapeDtypeStruct((b, hp, D_STATE), F32)],
        scratch_shapes=[pltpu.VMEM((hp, D_STATE), F32), pltpu.VMEM((8, CONV_DIM), F32)],
        compiler_params=_params("parallel", "arbitrary"),
        name="ssd_mixer",
    )(xbc, z, dt_raw, hist, h0, conv_w_t, conv_b, dt_bias, a_log, d_skip, norm_g)


def _outproj_body(*refs, n_cfg):
    o_refs = refs[:n_cfg]
    lse_refs = refs[n_cfg:2 * n_cfg] if n_cfg > 1 else ()
    ssm_ref, h_ref, w_ref, g_ref, b_ref, exp_ref, y_ref = refs[-7:]
    if n_cfg == 1:
        att = o_refs[0][...]
    else:
        lses = [r[...] for r in lse_refs]
        mx = functools.reduce(jnp.maximum, lses)
        es = [jnp.exp(l - mx) for l in lses]
        inv = 1.0 / functools.reduce(lambda a, c: a + c, es)
        att = None
        for o_ref, e in zip(o_refs, es):
            w = e * inv
            hi = w.astype(BF16)
            lo = (w - hi.astype(F32)).astype(BF16)
            wexp = (jnp.dot(hi, exp_ref[...], preferred_element_type=F32)
                    + jnp.dot(lo, exp_ref[...], preferred_element_type=F32))
            term = wexp * o_ref[...]
            att = term if att is None else att + term
    out = jnp.dot(att.astype(BF16), w_ref[:D_MODEL, :], preferred_element_type=F32)
    out = out + jnp.dot(ssm_ref[...].astype(BF16), w_ref[D_MODEL:, :], preferred_element_type=F32)
    y_ref[...] = _layer_norm(ALPHA * h_ref[...] + out, g_ref[...], b_ref[...])


def out_proj(att_parts, lse_parts, ssm, h, w_out_b, ln_g, ln_b, head_expand):
    m = h.shape[0]
    tm = min(256, m)
    n_cfg = len(att_parts)
    tok = lambda i: (i, 0)
    const = lambda i: (0, 0)
    ins = list(att_parts) + (list(lse_parts) if n_cfg > 1 else [])
    specs = [pl.BlockSpec((tm, D_MODEL), tok)] * n_cfg + ([pl.BlockSpec((tm, BLK), tok)] * n_cfg if n_cfg > 1 else [])
    return pl.pallas_call(
        functools.partial(_outproj_body, n_cfg=n_cfg),
        grid=(m // tm,),
        in_specs=specs + [pl.BlockSpec((tm, D_MODEL), tok), pl.BlockSpec((tm, D_MODEL), tok),
                          pl.BlockSpec((2 * D_MODEL, D_MODEL), const), pl.BlockSpec((1, D_MODEL), const),
                          pl.BlockSpec((1, D_MODEL), const), pl.BlockSpec((BLK, D_MODEL), const)],
        out_specs=pl.BlockSpec((tm, D_MODEL), tok),
        out_shape=jax.ShapeDtypeStruct((m, D_MODEL), F32),
        compiler_params=_params("parallel"),
        name=f"out_proj_c{n_cfg}",
    )(*ins, ssm, h, w_out_b, ln_g, ln_b, head_expand)


def _fold_body(sk_ref, wq_ref, o_ref):
    o_ref[0] = lax.dot_general(sk_ref[0], wq_ref[...], (((1,), (1,)), ((), ())),
                               preferred_element_type=F32, precision=lax.Precision.HIGHEST)


def fold_scores(peer_wq, peer_keys):
    dk = peer_keys.shape[-1]
    nblk = PEER_HEADS * 2
    folded = pl.pallas_call(
        _fold_body,
        grid=(nblk,),
        in_specs=[pl.BlockSpec((1, PEER_NKEYS, dk), lambda i: (i % 2, 0, 0)),
                  pl.BlockSpec((D_MODEL, dk), lambda i: (0, i))],
        out_specs=pl.BlockSpec((1, PEER_NKEYS, D_MODEL), lambda i: (i, 0, 0)),
        out_shape=jax.ShapeDtypeStruct((nblk, PEER_NKEYS, D_MODEL), F32),
        compiler_params=_params("parallel"),
        name="peer_fold",
    )(peer_keys, peer_wq)
    folded = folded.reshape(PEER_HEADS, 2, PEER_NKEYS, D_MODEL).transpose(1, 2, 0, 3)
    return folded.reshape(2 * PEER_NKEYS * PEER_HEADS, D_MODEL)


_CANDS = tuple((k, l) for k in range(PEER_TOPK) for l in range(PEER_TOPK) if (k + 1) * (l + 1) <= PEER_TOPK)


def _topk_extract(score_ref, work_ref, rank_ref, vals_ref):
    shape = work_ref.shape
    eio = lax.broadcasted_iota(jnp.int32, shape, 0)
    work_ref[...] = score_ref[...]
    rank_ref[...] = jnp.full(shape, 99.0, F32)

    def body(t, carry):
        sw = work_ref[...]
        m = jnp.max(sw, axis=0)
        idx = jnp.min(jnp.where(sw == m[None], eio, PEER_NKEYS), axis=0)
        sel = eio == idx[None]
        work_ref[...] = jnp.where(sel, NEG_INF, sw)
        rank_ref[...] = jnp.where(sel, jnp.asarray(t).astype(F32), rank_ref[...])
        vals_ref[t] = m
        return carry

    lax.fori_loop(0, PEER_TOPK, body, 0)


def _cmp_exchange(x, i, j):
    a, b = x[i], x[j]
    if b is None:
        return
    if a is None:
        x[i], x[j] = b, None
        return
    x[i], x[j] = jnp.maximum(a, b), jnp.minimum(a, b)


def _sort_desc(x):
    x = list(x)
    n = len(x)
    k = 2
    while k <= n:
        j = k // 2
        while j > 0:
            for i in range(n):
                l = i ^ j
                if l > i:
                    if (i & k) == 0:
                        _cmp_exchange(x, i, l)
                    else:
                        _cmp_exchange(x, l, i)
            j //= 2
        k *= 2
    return x


def _merge_top(a, b):
    n = len(a)
    m = []
    for i in range(n):
        u, v = a[i], b[n - 1 - i]
        m.append(v if u is None else (u if v is None else jnp.maximum(u, v)))
    j = n // 2
    while j > 0:
        for i in range(n):
            if (i & j) == 0:
                _cmp_exchange(m, i, i + j)
        j //= 2
    return m


def _top_sorted(tiles):
    groups = [_sort_desc(tiles[g:g + PEER_TOPK]) for g in range(0, len(tiles), PEER_TOPK)]
    while len(groups) > 1:
        groups = [_merge_top(groups[g], groups[g + 1]) for g in range(0, len(groups), 2)]
    return groups[0]


def _store_head_major(tmp_scr, val, out_ref):
    nk, nh = PEER_NKEYS, PEER_HEADS
    tmp_scr[...] = val.reshape(nk * nh, val.shape[-1])
    for h in range(nh):
        out_ref[h] = tmp_scr[pl.ds(h, nk, stride=nh), :].astype(BF16)


def _route_by_value(s_scr, cmap_ref, e1_ref, r2_ref, w2_ref, tmp_scr):
    topk = PEER_TOPK
    a = _top_sorted([s_scr[0, e] for e in range(PEER_NKEYS)])
    b = _top_sorted([s_scr[1, e] for e in range(PEER_NKEYS)])
    row_len = [topk // (k + 1) for k in range(topk)]
    sums = [[a[k] + b[l] for l in range(row_len[k])] for k in range(topk)]
    pad = lambda lst: list(lst) + [None] * (topk - len(lst))
    top = sums[0]
    lists = [sums[1], [sums[k][0] for k in range(topk // 2, topk)]] + [sums[k] for k in range(2, topk // 2)]
    for lst in lists:
        top = _merge_top(top, pad(lst))
    tau = top[topk - 1]
    z = None
    for t in range(topk):
        ez = jnp.exp(top[t] - top[0])
        z = ez if z is None else z + ez
    inv_z = 1.0 / z
    counts = []
    for k in range(topk):
        ck = None
        for l in range(row_len[k]):
            one = jnp.where(sums[k][l] >= tau, 1.0, 0.0)
            ck = one if ck is None else ck + one
        counts.append(ck)
    total = functools.reduce(lambda u, v: u + v, counts)
    floor1 = a[0]
    for k in range(1, topk):
        floor1 = jnp.where(counts[k] >= 1.0, a[k], floor1)

    s1 = s_scr[0]
    n1 = jnp.sum(jnp.where(s1 >= a[topk - 1][None], 1.0, 0.0), axis=0)
    cmap = jnp.where(s1 >= floor1[None], 1.0, 0.0)
    for k in range(topk // 2):
        cmap = jnp.where(s1 == a[k][None], counts[k][None], cmap)
    cmap_ref[...] = cmap
    e1_ref[...] = jnp.exp(s1 - a[0][None])
    s2 = s_scr[1]
    n2 = jnp.sum(jnp.where(s2 >= b[topk - 1][None], 1.0, 0.0), axis=0)
    rank2 = jnp.full(s2.shape, 99.0, F32)
    for l in range(topk):
        rank2 = jnp.where(s2 == b[l][None], float(l), rank2)
    _store_head_major(tmp_scr, rank2, r2_ref)
    _store_head_major(tmp_scr, jnp.exp(s2 - b[0][None]) * inv_z[None], w2_ref)
    ftop = float(topk)
    return ((total != ftop) | ((n1 > ftop) & (counts[topk - 1] >= 1.0)) | ((n2 > ftop) & (counts[0] >= ftop)))


def _route_body(h_ref, ws_ref, cmap_ref, e1_ref, r2_ref, w2_ref,
                s_scr, work_scr, rank1_scr, rank2_scr, a_scr, b_scr, tmp_scr):
    tl = s_scr.shape[-1]
    nk, nh = PEER_NKEYS, PEER_HEADS
    s_t = lax.dot_general(ws_ref[...], h_ref[...].astype(BF16), (((1,), (1,)), ((), ())),
                          preferred_element_type=F32)
    s_scr[...] = s_t.reshape(2, nk, nh, tl)
    unsure = _route_by_value(s_scr, cmap_ref, e1_ref, r2_ref, w2_ref, tmp_scr)

    @pl.when(jnp.max(jnp.where(unsure, 1.0, 0.0)) > 0.0)
    def _():
        _route_by_index(s_scr, cmap_ref, e1_ref, r2_ref, w2_ref,
                        work_scr, rank1_scr, rank2_scr, a_scr, b_scr, tmp_scr)


def _route_by_index(s_scr, cmap_ref, e1_ref, r2_ref, w2_ref,
                    work_scr, rank1_scr, rank2_scr, a_scr, b_scr, tmp_scr):
    _topk_extract(s_scr.at[0], work_scr, rank1_scr, a_scr)
    _topk_extract(s_scr.at[1], work_scr, rank2_scr, b_scr)

    cvals = [a_scr[k] + b_scr[l] for k, l in _CANDS]
    nc = len(_CANDS)
    after = [None] * nc
    before = [None] * nc
    for x in range(nc):
        for y in range(x + 1, nc):
            gt = jnp.where(cvals[y] > cvals[x], 1.0, 0.0)
            after[x] = gt if after[x] is None else after[x] + gt
            before[y] = gt if before[y] is None else before[y] + gt
    z = None
    counts = [None] * PEER_TOPK
    for x, (k, l) in enumerate(_CANDS):
        ahead = float(x)
        if after[x] is not None:
            ahead = ahead + after[x]
        if before[x] is not None:
            ahead = ahead - before[x]
        chosen = ahead < float(PEER_TOPK)
        one = jnp.where(chosen, 1.0, 0.0)
        counts[k] = one if counts[k] is None else counts[k] + one
        ez = jnp.where(chosen, jnp.exp(cvals[x] - cvals[0]), 0.0)
        z = ez if z is None else z + ez
    inv_z = 1.0 / z

    rank1 = rank1_scr[...]
    cmap = jnp.zeros(rank1.shape, F32)
    for k in range(PEER_TOPK):
        cmap = jnp.where(rank1 == float(k), counts[k][None], cmap)
    cmap_ref[...] = cmap
    e1_ref[...] = jnp.where(rank1 < float(PEER_TOPK), jnp.exp(s_scr[0] - a_scr[0][None]), 0.0)
    rank2 = rank2_scr[...]
    w2 = jnp.where(rank2 < float(PEER_TOPK), jnp.exp(s_scr[1] - b_scr[0][None]) * inv_z[None], 0.0)
    _store_head_major(tmp_scr, rank2, r2_ref)
    _store_head_major(tmp_scr, w2, w2_ref)


def peer_route(h2d, ws_t):
    m = h2d.shape[0]
    tl = 128
    nk, nh = PEER_NKEYS, PEER_HEADS
    tok3 = lambda i: (0, 0, i)
    return pl.pallas_call(
        _route_body,
        grid=(m // tl,),
        in_specs=[pl.BlockSpec((tl, D_MODEL), lambda i: (i, 0)),
                  pl.BlockSpec((2 * nk * nh, D_MODEL), lambda i: (0, 0))],
        out_specs=[pl.BlockSpec((nk, nh, tl), tok3), pl.BlockSpec((nk, nh, tl), tok3),
                   pl.BlockSpec((nh, nk, tl), tok3), pl.BlockSpec((nh, nk, tl), tok3)],
        out_shape=[jax.ShapeDtypeStruct((nk, nh, m), F32), jax.ShapeDtypeStruct((nk, nh, m), F32),
                   jax.ShapeDtypeStruct((nh, nk, m), BF16), jax.ShapeDtypeStruct((nh, nk, m), BF16)],
        scratch_shapes=[pltpu.VMEM((2, nk, nh, tl), F32), pltpu.VMEM((nk, nh, tl), F32),
                        pltpu.VMEM((nk, nh, tl), F32), pltpu.VMEM((nk, nh, tl), F32),
                        pltpu.VMEM((PEER_TOPK, nh, tl), F32), pltpu.VMEM((PEER_TOPK, nh, tl), F32),
                        pltpu.VMEM((nk * nh, tl), F32)],
        compiler_params=_params("parallel"),
        name="peer_route",
    )(h2d, ws_t)


def _gated_hidden(i, ib, a_scr, hid_scr, cmap_ref, e1_ref, r2_ref, w2_ref):
    tm = a_scr.shape[-1]
    sub = 16
    grp = (PEER_NKEYS // sub, sub, tm)
    blk = slice(ib * PEER_NKEYS, (ib + 1) * PEER_NKEYS)
    gate = jnp.zeros(grp, BF16)
    for h in range(PEER_HEADS):
        cm = jnp.broadcast_to(cmap_ref[i, h:h + 1, :], (sub, tm)).astype(BF16)
        e1 = jnp.broadcast_to(e1_ref[i, h:h + 1, :], (sub, tm)).astype(BF16)
        w2 = w2_ref[h].reshape(grp)
        gate = gate + jnp.where(r2_ref[h].reshape(grp) < cm[None], w2, jnp.zeros_like(w2)) * e1[None]
    a = a_scr[blk, :]
    act = 0.5 * a * (1.0 + lax.erf(a * (2.0 ** -0.5)))
    hid_scr[blk, :] = gate.reshape(PEER_NKEYS, tm) * act.astype(BF16)


def _experts_body(h_ref, u_ref, vt_ref, cmap_ref, e1_ref, r2_ref, w2_ref, g_ref, b_ref, y_ref,
                  acc_ref, a_scr, hid_scr, ht_scr, *, rows):
    j = pl.program_id(1)

    @pl.when(j == 0)
    def _():
        acc_ref[...] = jnp.zeros_like(acc_ref)
        ht_scr[...] = h_ref[...].T.astype(BF16)

    a_scr[...] = jnp.dot(u_ref[...], ht_scr[...], preferred_element_type=F32)
    for ib in range(rows):
        _gated_hidden(ib, ib, a_scr, hid_scr, cmap_ref, e1_ref, r2_ref, w2_ref)
    acc_ref[...] += jnp.dot(vt_ref[...], hid_scr[...], preferred_element_type=F32)

    @pl.when(j == pl.num_programs(1) - 1)
    def _():
        y_ref[...] = _layer_norm(ALPHA * h_ref[...] + acc_ref[...].T, g_ref[...], b_ref[...])


def peer_experts(h2d, u_b, vt_b, cmap, e1map, rank2, w2, ln_g, ln_b):
    m = h2d.shape[0]
    tm = min(1024, m)
    te = 1024
    rows = te // PEER_NKEYS
    nk, nh = PEER_NKEYS, PEER_HEADS
    tok = lambda i, j: (i, 0)
    tok3 = lambda i, j: (0, 0, i)
    const = lambda i, j: (0, 0)
    return pl.pallas_call(
        functools.partial(_experts_body, rows=rows),
        grid=(m // tm, u_b.shape[0] // te),
        in_specs=[pl.BlockSpec((tm, D_MODEL), tok),
                  pl.BlockSpec((te, D_MODEL), lambda i, j: (j, 0)),
                  pl.BlockSpec((D_MODEL, te), lambda i, j: (0, j)),
                  pl.BlockSpec((rows, nh, tm), lambda i, j: (j, 0, i)),
                  pl.BlockSpec((rows, nh, tm), lambda i, j: (j, 0, i)),
                  pl.BlockSpec((nh, nk, tm), tok3), pl.BlockSpec((nh, nk, tm), tok3),
                  pl.BlockSpec((1, D_MODEL), const), pl.BlockSpec((1, D_MODEL), const)],
        out_specs=pl.BlockSpec((tm, D_MODEL), tok),
        out_shape=jax.ShapeDtypeStruct((m, D_MODEL), F32),
        scratch_shapes=[pltpu.VMEM((D_MODEL, tm), F32), pltpu.VMEM((te, tm), F32), pltpu.VMEM((te, tm), BF16),
                        pltpu.VMEM((D_MODEL, tm), BF16)],
        compiler_params=_params("parallel", "arbitrary"),
        name="peer_experts",
    )(h2d, u_b, vt_b, cmap, e1map, rank2, w2, ln_g, ln_b)


def _transpose_body(x_ref, o_ref):
    o_ref[...] = x_ref[...].T.astype(BF16)


def transpose_cast(x):
    n, d = x.shape
    tn = 512
    return pl.pallas_call(
        _transpose_body,
        grid=(n // tn,),
        in_specs=[pl.BlockSpec((tn, d), lambda i: (i, 0))],
        out_specs=pl.BlockSpec((d, tn), lambda i: (0, i)),
        out_shape=jax.ShapeDtypeStruct((d, n), BF16),
        compiler_params=_params("parallel"),
        name="transpose_cast",
    )(x)


def peer_layer(h2d, ws_t, u_b, vt_b, ln_g, ln_b):
    cmap, e1map, rank2, w2 = peer_route(h2d, ws_t)
    return peer_experts(h2d, u_b, vt_b, cmap, e1map, rank2, w2, ln_g, ln_b)


def kernel(x_prompt, x_sample, cache_k, cache_v, state_conv, state_ssm, w_in, conv_w, conv_b, dt_bias, a_log,
           d_skip, ssm_norm_g, w_out, ln1_g, ln1_b, peer_wq, peer_keys, peer_u, peer_v, ln2_g, ln2_b):
    bp, sp, d = x_prompt.shape
    bs, ts, _ = x_sample.shape
    w_buf = cache_k.shape[2]
    hp = N_HEADS * HEAD_DIM
    tq = 8
    hpv = x_prompt.reshape(bp * sp, d)
    hsv = x_sample.reshape(bs * ts, d)
    lane_head = jnp.arange(D_MODEL)[None, :] // HEAD_DIM
    head_expand = (jnp.arange(BLK)[:, None] == lane_head).astype(BF16)
    pad_lanes = lambda a: jnp.pad(a[None, :], ((0, 0), (0, DT_PAD - a.shape[0])))
    outs = {n: [] for n in ("pk", "pv", "pc", "ps", "sc", "ss")}
    assert w_buf == MAX_WINDOW, "the new window drops exactly the ts oldest keys"
    to_feature_major = lambda a: jnp.transpose(a, (0, 1, 3, 4, 2)).reshape(DEPTH, bs, d, w_buf)
    from_feature_major = lambda a: jnp.transpose(a.reshape(DEPTH, bs, N_HEADS, HEAD_DIM, w_buf), (0, 1, 4, 2, 3))
    cache_kt, cache_vt = to_feature_major(cache_k), to_feature_major(cache_v)
    new_kt = new_vt = None

    for l in range(DEPTH):
        w_pad = jnp.pad(w_in[l], ((0, 0), (0, DT_PAD - N_HEADS))).astype(BF16)
        w_out_b = w_out[l].astype(BF16)
        cw_t = conv_w[l].T
        cb = conv_b[l][None, :]
        dtb, alog = pad_lanes(dt_bias[l]), pad_lanes(a_log[l])
        ng = ssm_norm_g[l][None, :]
        g1, b1, g2, b2 = ln1_g[l][None, :], ln1_b[l][None, :], ln2_g[l][None, :], ln2_b[l][None, :]
        ws_t = fold_scores(peer_wq[l], peer_keys[l]).astype(BF16)
        u_b = peer_u[l].astype(BF16)
        vt_b = transpose_cast(peer_v[l])

        q, k, v, z, xbc, dtr = in_proj(hpv, w_pad)
        q3, k3, v3 = (a.reshape(bp, sp, d) for a in (q, k, v))
        att = [band_attention(q3, k3, v3, dil) for _, dil in DILATED_CONFIGS]
        y, h_last = ssd_mixer(xbc.reshape(bp, sp, CONV_DIM), z.reshape(bp, sp, d), dtr.reshape(bp, sp, DT_PAD),
                              jnp.zeros((bp, 8, CONV_DIM), F32), jnp.zeros((bp, hp, D_STATE), F32),
                              cw_t, cb, dtb, alog, d_skip[l], ng, valid=BLK)
        hpv = out_proj([o.reshape(bp * sp, d) for o, _ in att], [s.reshape(bp * sp, BLK) for _, s in att],
                       y.reshape(bp * sp, d), hpv, w_out_b, g1, b1, head_expand)
        hpv = peer_layer(hpv, ws_t, u_b, vt_b, g2, b2)
        keep = max(0, sp - MAX_WINDOW)
        outs["pk"].append(k3[:, keep:].reshape(bp, sp - keep, N_HEADS, HEAD_DIM))
        outs["pv"].append(v3[:, keep:].reshape(bp, sp - keep, N_HEADS, HEAD_DIM))
        outs["pc"].append(xbc.reshape(bp, sp, CONV_DIM)[:, sp - (CONV_WIDTH - 1):])
        outs["ps"].append(h_last.reshape(bp, N_HEADS, HEAD_DIM, D_STATE))

        q, k, v, z, xbc, dtr = in_proj(hsv, w_pad)
        pad_t = lambda a, n: jnp.pad(a.reshape(bs, ts, a.shape[-1]), ((0, 0), (0, n - ts), (0, 0)))
        att_s, new_kt, new_vt = cache_attention(pad_t(q, tq), pad_t(k, tq), pad_t(v, tq), cache_kt, cache_vt, l,
                                                None if l == 0 else (new_kt, new_vt), ts)
        att_s = att_s[:, :ts].reshape(bs * ts, d)
        hist = jnp.pad(state_conv[l], ((0, 0), (8 - (CONV_WIDTH - 1), 0), (0, 0)))
        y, h_last = ssd_mixer(pad_t(xbc, BLK), pad_t(z, BLK), pad_t(dtr, BLK), hist,
                              state_ssm[l].reshape(bs, hp, D_STATE), cw_t, cb, dtb, alog, d_skip[l], ng, valid=ts)
        y = y[:, :ts].reshape(bs * ts, d)
        hsv = out_proj([att_s], [], y, hsv, w_out_b, g1, b1, head_expand)
        hsv = peer_layer(hsv, ws_t, u_b, vt_b, g2, b2)
        conv_ext = jnp.concatenate([state_conv[l], xbc.reshape(bs, ts, CONV_DIM)], axis=1)
        outs["sc"].append(conv_ext[:, -(CONV_WIDTH - 1):])
        outs["ss"].append(h_last.reshape(bs, N_HEADS, HEAD_DIM, D_STATE))

    st = {n: jnp.stack(v) for n, v in outs.items()}
    return (hpv.reshape(bp, sp, d), hsv.reshape(bs, ts, d), st["pk"], st["pv"], st["pc"], st["ps"],
            from_feature_major(new_kt), from_feature_major(new_vt), st["sc"], st["ss"])
```
